```python
import math
import jax, jax.numpy as jnp
from jax import lax
import numpy as np

D_MODEL = 2048
BATCH = 4
SEQ = 2048
DEPTH = 1
DEC_BATCH = 8
DEC_SEQ = 4
PAST_LEN = 16384
PAGE_SIZE = 128

N_META = 16
ATT_WIDTH = D_MODEL // 2
HEAD_DIM_A = 64
V_DIM_A = 2 * HEAD_DIM_A
N_HEADS_A = ATT_WIDTH // V_DIM_A
MLSTM_WIDTH = D_MODEL - ATT_WIDTH
N_HEADS_M = 4
HEAD_DIM_M = MLSTM_WIDTH // N_HEADS_M
CONV_W = 4
CHUNK = 64
Q_BLOCK = 128
D_FF = 4 * D_MODEL
PROJ_WIDTH = 3 * ATT_WIDTH + 3 * MLSTM_WIDTH + 2 * N_HEADS_M
FORGET_BIAS = 3.0
EPS = 1e-6

kernel_name = "hymba_diffattn_mlstm_decoder_step"


def rms_norm(x, g):
    xf = x.astype(jnp.float32)
    y = xf * lax.rsqrt(jnp.mean(xf * xf, axis=-1, keepdims=True) + EPS)
    return (y * g.astype(jnp.float32)).astype(x.dtype)


def alibi_slopes():
    return 2.0 ** (-8.0 * jnp.arange(1, N_HEADS_A + 1, dtype=jnp.float32) / N_HEADS_A)


def diff_lambda(lambda_qk, lam_init):
    l = lambda_qk.astype(jnp.float32)
    return jnp.exp(jnp.sum(l[0] * l[1])) - jnp.exp(jnp.sum(l[2] * l[3])) + lam_init


def mixer_projections(x, norm_g, w_in, q_g, k_g):
    B, T, _ = x.shape
    z = rms_norm(x, norm_g) @ w_in
    widths = [ATT_WIDTH] * 3 + [MLSTM_WIDTH] * 3 + [N_HEADS_M]
    offs = []
    acc = 0
    for w in widths:
        acc += w
        offs.append(acc)
    qa, ka, va, u, vm, og, gi, gf = jnp.split(z, offs, axis=-1)
    qa = rms_norm(qa.reshape(B, T, N_HEADS_A, 2, HEAD_DIM_A), q_g)
    ka = rms_norm(ka.reshape(B, T, N_HEADS_A, 2, HEAD_DIM_A), k_g)
    va = va.reshape(B, T, N_HEADS_A, V_DIM_A)
    return qa, ka, va, u, vm, og, gi, gf


def diff_attend(q, k, v, q_pos, k_pos, lam, slopes):
    s = jnp.einsum("bqhmd,bkhmd->bhmqk", q, k, preferred_element_type=jnp.float32) * (HEAD_DIM_A ** -0.5)
    dist = q_pos[:, None] - k_pos[None, :]
    bias = -slopes[:, None, None] * dist.astype(jnp.float32)[None]
    s = jnp.where((dist >= 0)[None, None, None], s + bias[None, :, None], -jnp.inf)
    p = jax.nn.softmax(s, axis=-1)
    a = p[:, :, 0] - lam * p[:, :, 1]
    return jnp.einsum("bhqk,bkhd->bqhd", a.astype(v.dtype), v)


def prompt_attention(qa, ka, va, lam, slopes):
    B, T = qa.shape[:2]
    pos = jnp.arange(T)
    out_meta = diff_attend(qa[:, :N_META], ka[:, :N_META], va[:, :N_META], pos[:N_META], pos[:N_META], lam, slopes)
    n_blk = (T - N_META) // Q_BLOCK
    q_blocks = jnp.moveaxis(qa[:, N_META:].reshape(B, n_blk, Q_BLOCK, N_HEADS_A, 2, HEAD_DIM_A), 1, 0)
    p_blocks = pos[N_META:].reshape(n_blk, Q_BLOCK)
    out_blocks = lax.map(lambda qp: diff_attend(qp[0], ka, va, qp[1], pos, lam, slopes), (q_blocks, p_blocks))
    out_real = jnp.moveaxis(out_blocks, 0, 1).reshape(B, T - N_META, N_HEADS_A, V_DIM_A)
    return jnp.concatenate([out_meta, out_real], axis=1)


def mlstm_features(u, vm, gi, gf, conv_buf, conv_w, conv_b, w_qm, w_km, gate_bias):
    B, T, _ = u.shape
    up = jnp.concatenate([conv_buf.astype(u.dtype), u], axis=1)
    conv = conv_b
    for j in range(CONV_W):
        conv = conv + up[:, j:j + T] * conv_w[j]
    ua = jax.nn.silu(conv).reshape(B, T, N_HEADS_M, HEAD_DIM_M)
    q = jnp.einsum("bthd,hde->bthe", ua, w_qm).astype(jnp.float32)
    k = (jnp.einsum("bthd,hde->bthe", ua, w_km) * HEAD_DIM_M ** -0.5).astype(jnp.float32)
    v = vm.reshape(B, T, N_HEADS_M, HEAD_DIM_M).astype(jnp.float32)
    ig = (gi + gate_bias[0]).astype(jnp.float32)
    lf = jax.nn.log_sigmoid((gf + gate_bias[1]).astype(jnp.float32))
    return (q, k, v, ig, lf), up[:, -(CONV_W - 1):]


def mlstm_chunk(state, feats):
    C0, n0, m0 = state
    q, k, v, ig, lf = feats
    L = q.shape[1]
    b = jnp.moveaxis(jnp.cumsum(lf, axis=1), 1, -1)
    i_ = jnp.moveaxis(ig, 1, -1)
    causal = jnp.tril(jnp.ones((L, L), dtype=bool))
    d = jnp.where(causal, b[..., :, None] - b[..., None, :] + i_[..., None, :], -jnp.inf)
    inter = b + m0[..., None]
    m = jnp.maximum(inter, jnp.max(d, axis=-1))
    w_inter = jnp.exp(inter - m)
    s = jnp.einsum("bthd,bshd->bhts", q, k) * jnp.exp(d - m[..., None])
    num = w_inter[..., None] * jnp.einsum("bthd,bhde->bhte", q, C0) + jnp.einsum("bhts,bshe->bhte", s, v)
    den = w_inter * jnp.einsum("bthd,bhd->bht", q, n0) + jnp.sum(s, axis=-1)
    h = num / jnp.maximum(jnp.abs(den), jnp.exp(-m))[..., None]
    m_end = m[..., -1]
    w_c = jnp.exp(b[..., -1] + m0 - m_end)
    w_s = jnp.exp(b[..., -1:] - b + i_ - m_end[..., None])
    C = w_c[..., None, None] * C0 + jnp.einsum("bhs,bshd,bshe->bhde", w_s, k, v)
    n = w_c[..., None] * n0 + jnp.einsum("bhs,bshd->bhd", w_s, k)
    return (C, n, m_end), jnp.moveaxis(h, 1, 2)


def prompt_mlstm(feats):
    q = feats[0]
    B = q.shape[0]
    state = (jnp.zeros((B, N_HEADS_M, HEAD_DIM_M, HEAD_DIM_M), jnp.float32),
             jnp.zeros((B, N_HEADS_M, HEAD_DIM_M), jnp.float32),
             jnp.zeros((B, N_HEADS_M), jnp.float32))
    state, h_meta = mlstm_chunk(state, tuple(a[:, :N_META] for a in feats))
    def to_chunks(a):
        r = a[:, N_META:]
        nc = r.shape[1] // CHUNK
        return jnp.moveaxis(r.reshape(B, nc, CHUNK, *r.shape[2:]), 1, 0)
    state, h_chunks = lax.scan(mlstm_chunk, state, tuple(to_chunks(a) for a in feats))
    h_real = jnp.moveaxis(h_chunks, 0, 1).reshape(B, -1, N_HEADS_M, HEAD_DIM_M)
    return jnp.concatenate([h_meta, h_real], axis=1), state


def mlstm_output(h, og, g):
    B, T = h.shape[:2]
    hn = rms_norm(h, g)
    o = jax.nn.sigmoid(og.astype(jnp.float32)).reshape(B, T, N_HEADS_M, HEAD_DIM_M)
    return (hn * o).reshape(B, T, MLSTM_WIDTH).astype(og.dtype)


def finish_layer(x, a_heads, m_out, lam_init, attn_out_norm, w_out, norm_ffn, w_up, w_down):
    B, T, _ = x.shape
    a = (rms_norm(a_heads, attn_out_norm) * (1.0 - lam_init)).reshape(B, T, ATT_WIDTH)
    x = x + jnp.concatenate([a.astype(x.dtype), m_out.astype(x.dtype)], axis=-1) @ w_out
    hf = rms_norm(x, norm_ffn) @ w_up
    return x + jnp.square(jax.nn.relu(hf)) @ w_down


def setup_inputs(seed: int = 0) -> dict:
    key = jax.random.key(seed)
    ks = jax.random.split(key, 28)
    n_pages = PAST_LEN // PAGE_SIZE
    n_pool = (DEC_BATCH * n_pages * 5) // 4
    def nrm(k, shape, scale=1.0):
        return jax.random.normal(k, shape, jnp.float32) * scale
    perm = jax.random.permutation(ks[8], n_pool)
    page_table = perm[: DEC_BATCH * n_pages].reshape(DEC_BATCH, n_pages).astype(jnp.int32)
    return {
        "x_prompt": nrm(ks[0], (BATCH, SEQ, D_MODEL)),
        "x_sample": nrm(ks[1], (DEC_BATCH, DEC_SEQ, D_MODEL)),
        "cache_k": nrm(ks[2], (DEPTH, n_pool, PAGE_SIZE, N_HEADS_A, 2 * HEAD_DIM_A)),
        "cache_v": nrm(ks[3], (DEPTH, n_pool, PAGE_SIZE, N_HEADS_A, V_DIM_A)),
        "state_C": nrm(ks[4], (DEPTH, DEC_BATCH, N_HEADS_M, HEAD_DIM_M, HEAD_DIM_M), 0.5),
        "state_n": nrm(ks[5], (DEPTH, DEC_BATCH, N_HEADS_M, HEAD_DIM_M), 0.5),
        "state_m": 1.0 + nrm(ks[6], (DEPTH, DEC_BATCH, N_HEADS_M), 0.5),
        "state_conv": nrm(ks[7], (DEPTH, DEC_BATCH, CONV_W - 1, MLSTM_WIDTH)),
        "page_table": page_table,
        "meta_tokens": nrm(ks[9], (N_META, D_MODEL)),
        "norm_mix": 1.0 + nrm(ks[10], (DEPTH, D_MODEL), 0.02),
        "w_in": nrm(ks[11], (DEPTH, D_MODEL, PROJ_WIDTH), D_MODEL ** -0.5),
        "q_norm": 1.0 + nrm(ks[12], (DEPTH, HEAD_DIM_A), 0.02),
        "k_norm": 1.0 + nrm(ks[13], (DEPTH, HEAD_DIM_A), 0.02),
        "lambda_qk": nrm(ks[14], (DEPTH, 4, HEAD_DIM_A), 0.1),
        "attn_out_norm": 1.0 + nrm(ks[15], (DEPTH, V_DIM_A), 0.02),
        "conv_w": nrm(ks[16], (DEPTH, CONV_W, MLSTM_WIDTH), CONV_W ** -0.5),
        "conv_b": nrm(ks[17], (DEPTH, MLSTM_WIDTH), 0.02),
        "w_qm": nrm(ks[18], (DEPTH, N_HEADS_M, HEAD_DIM_M, HEAD_DIM_M), HEAD_DIM_M ** -0.5),
        "w_km": nrm(ks[19], (DEPTH, N_HEADS_M, HEAD_DIM_M, HEAD_DIM_M), HEAD_DIM_M ** -0.5),
        "gate_bias": jnp.stack([nrm(ks[20], (DEPTH, N_HEADS_M), 0.1),
                                FORGET_BIAS + nrm(ks[21], (DEPTH, N_HEADS_M), 0.1)], axis=1),
        "mlstm_out_norm": 1.0 + nrm(ks[22], (DEPTH, HEAD_DIM_M), 0.02),
        "w_out": nrm(ks[23], (DEPTH, D_MODEL, D_MODEL), D_MODEL ** -0.5),
        "norm_ffn": 1.0 + nrm(ks[24], (DEPTH, D_MODEL), 0.02),
        "w_up": nrm(ks[25], (DEPTH, D_MODEL, D_FF), D_MODEL ** -0.5),
        "w_down": nrm(ks[26], (DEPTH, D_FF, D_MODEL), D_FF ** -0.5),
    }


def reference(x_prompt, x_sample, cache_k, cache_v, state_C, state_n, state_m, state_conv, page_table,
              meta_tokens, norm_mix, w_in, q_norm, k_norm, lambda_qk, attn_out_norm, conv_w, conv_b,
              w_qm, w_km, gate_bias, mlstm_out_norm, w_out, norm_ffn, w_up, w_down):
    slopes = alibi_slopes()
    B, S, _ = x_prompt.shape
    Bd, Ts, _ = x_sample.shape
    past_len = page_table.shape[1] * PAGE_SIZE
    T = S + N_META
    xp = jnp.concatenate([jnp.broadcast_to(meta_tokens.astype(x_prompt.dtype)[None], (B, N_META, D_MODEL)), x_prompt], axis=1)
    xs = x_sample
    pos_q_s = past_len + jnp.arange(Ts)
    pos_k_s = jnp.arange(past_len + Ts)
    kp, vp, Cp, np_, mp, cp = [], [], [], [], [], []
    ksl, vsl, Csl, nsl, msl, csl = [], [], [], [], [], []
    for l in range(DEPTH):
        lam_init = 0.8 - 0.6 * math.exp(-0.3 * l)
        lam = diff_lambda(lambda_qk[l], lam_init)
        qa, ka, va, u, vm, og, gi, gf = mixer_projections(xp, norm_mix[l], w_in[l], q_norm[l], k_norm[l])
        a_p = prompt_attention(qa, ka, va, lam, slopes)
        feats, conv_p = mlstm_features(u, vm, gi, gf, jnp.zeros((B, CONV_W - 1, MLSTM_WIDTH), u.dtype),
                                       conv_w[l], conv_b[l], w_qm[l], w_km[l], gate_bias[l])
        h_p, (C_p, n_p, m_p) = prompt_mlstm(feats)
        xp = finish_layer(xp, a_p, mlstm_output(h_p, og, mlstm_out_norm[l]), lam_init,
                          attn_out_norm[l], w_out[l], norm_ffn[l], w_up[l], w_down[l])
        kp.append(ka.reshape(B, T, N_HEADS_A, 2 * HEAD_DIM_A))
        vp.append(va)
        Cp.append(C_p)
        np_.append(n_p)
        mp.append(m_p)
        cp.append(conv_p)
        qs, kss, vs, us, vms, ogs, gis, gfs = mixer_projections(xs, norm_mix[l], w_in[l], q_norm[l], k_norm[l])
        past_k = cache_k[l, page_table].reshape(Bd, past_len, N_HEADS_A, 2, HEAD_DIM_A)
        past_v = cache_v[l, page_table].reshape(Bd, past_len, N_HEADS_A, V_DIM_A)
        k_all = jnp.concatenate([past_k.astype(kss.dtype), kss], axis=1)
        v_all = jnp.concatenate([past_v.astype(vs.dtype), vs], axis=1)
        a_s = diff_attend(qs, k_all, v_all, pos_q_s, pos_k_s, lam, slopes)
        feats_s, conv_s = mlstm_features(us, vms, gis, gfs, state_conv[l], conv_w[l], conv_b[l],
                                         w_qm[l], w_km[l], gate_bias[l])
        st = (state_C[l].astype(jnp.float32), state_n[l].astype(jnp.float32), state_m[l].astype(jnp.float32))
        (C_s, n_s, m_s), h_s = mlstm_chunk(st, feats_s)
        xs = finish_layer(xs, a_s, mlstm_output(h_s, ogs, mlstm_out_norm[l]), lam_init,
                          attn_out_norm[l], w_out[l], norm_ffn[l], w_up[l], w_down[l])
        ksl.append(kss.reshape(Bd, Ts, N_HEADS_A, 2 * HEAD_DIM_A))
        vsl.append(vs)
        Csl.append(C_s)
        nsl.append(n_s)
        msl.append(m_s)
        csl.append(conv_s)
    y_prompt = xp[:, N_META:]
    y_sample = xs
    return (y_prompt, y_sample,
            jnp.stack(kp), jnp.stack(vp), jnp.stack(Cp), jnp.stack(np_), jnp.stack(mp), jnp.stack(cp),
            jnp.stack(ksl), jnp.stack(vsl), jnp.stack(Csl), jnp.stack(nsl), jnp.stack(msl), jnp.stack(csl))
```

```python
import functools
import math

import jax
import jax.numpy as jnp
from jax import lax
from jax.experimental import pallas as pl
from jax.experimental.pallas import tpu as pltpu

F32 = jnp.float32
BF16 = jnp.bfloat16

N_META = 16
HEAD_DIM_A = 64
V_DIM_A = 2 * HEAD_DIM_A
N_HEADS_M = 4
CONV_W = 4
PAGE_SIZE = 128
EPS = 1e-6

LANES = 128
SUBLANES = 8
VMEM_LIMIT_BYTES = 56 * 1024 * 1024

LOG2E = 1.4426950408889634
NEG_BIG = -1e30

SAMPLE_PAD = SUBLANES
MLSTM_CHUNK = 256
ATT_BLOCK = 512
ATT_HEADS_PER_STEP = 2
DEC_PAGES_PER_STEP = 8


def _dot(a, b):
    return jnp.dot(a, b, preferred_element_type=F32)


def _dot_nt(a, b):
    return lax.dot_general(a, b, (((1,), (1,)), ((), ())), preferred_element_type=F32)


def _dot_tn(a, b):
    return lax.dot_general(a, b, (((0,), (0,)), ((), ())), preferred_element_type=F32)


def _rms(x, g):
    return x * lax.rsqrt(jnp.mean(x * x, axis=-1, keepdims=True) + EPS) * g


def _params(*sem):
    return pltpu.CompilerParams(dimension_semantics=sem, vmem_limit_bytes=VMEM_LIMIT_BYTES)


def _inproj_kernel(x_ref, g_ref, w_ref, wgh_ref, wgl_ref, z_ref, gate_ref, xn_scr):
    @pl.when(pl.program_id(1) == 0)
    def _():
        xn = _rms(x_ref[...], g_ref[...])
        xh = xn.astype(BF16)
        xl = (xn - xh.astype(F32)).astype(BF16)
        xn_scr[...] = xh
        wgh = wgh_ref[...]
        gate_ref[...] = _dot(xh, wgh) + (_dot(xl, wgh) + _dot(xh, wgl_ref[...]))

    z_ref[...] = _dot(xn_scr[...], w_ref[...])


def _inproj(x, g, w, wgh, wgl, tm, tn):
    rows, d = x.shape
    n = w.shape[1]
    return pl.pallas_call(
        _inproj_kernel,
        grid=(rows // tm, n // tn),
        in_specs=[
            pl.BlockSpec((tm, d), lambda i, j: (i, 0)),
            pl.BlockSpec((1, d), lambda i, j: (0, 0)),
            pl.BlockSpec((d, tn), lambda i, j: (0, j)),
            pl.BlockSpec((d, LANES), lambda i, j: (0, 0)),
            pl.BlockSpec((d, LANES), lambda i, j: (0, 0)),
        ],
        out_specs=[
            pl.BlockSpec((tm, tn), lambda i, j: (i, j)),
            pl.BlockSpec((tm, LANES), lambda i, j: (i, 0)),
        ],
        out_shape=[jax.ShapeDtypeStruct((rows, n), F32), jax.ShapeDtypeStruct((rows, LANES), F32)],
        scratch_shapes=[pltpu.VMEM((tm, d), BF16)],
        compiler_params=_params("parallel", "arbitrary"),
        name="inproj",
    )(x, g, w, wgh, wgl)


def _outproj_kernel(x_ref, a_ref, m_ref, wa_ref, wm_ref, o_ref):
    a = a_ref[...].astype(BF16)
    m = m_ref[...].astype(BF16)
    o_ref[...] = x_ref[...] + (_dot(a, wa_ref[...]) + _dot(m, wm_ref[...]))


def _outproj(x, a, m, wa, wm, tm, tn):
    rows, d = x.shape
    ka = a.shape[1]
    km = m.shape[1]
    return pl.pallas_call(
        _outproj_kernel,
        grid=(rows // tm, d // tn),
        in_specs=[
            pl.BlockSpec((tm, tn), lambda i, j: (i, j)),
            pl.BlockSpec((tm, ka), lambda i, j: (i, 0)),
            pl.BlockSpec((tm, km), lambda i, j: (i, 0)),
            pl.BlockSpec((ka, tn), lambda i, j: (0, j)),
            pl.BlockSpec((km, tn), lambda i, j: (0, j)),
        ],
        out_specs=pl.BlockSpec((tm, tn), lambda i, j: (i, j)),
        out_shape=jax.ShapeDtypeStruct((rows, d), F32),
        compiler_params=_params("parallel", "arbitrary"),
        name="outproj",
    )(x, a, m, wa, wm)


def _ffn_kernel(x_ref, g_ref, wu_ref, wd_ref, o_ref, xn_scr):
    @pl.when(pl.program_id(1) == 0)
    def _():
        x = x_ref[...]
        xn_scr[...] = _rms(x, g_ref[...]).astype(BF16)
        o_ref[...] = x

    h = _dot(xn_scr[...], wu_ref[...])
    h = jnp.square(jnp.maximum(h, 0.0))
    o_ref[...] += _dot(h.astype(BF16), wd_ref[...])


def _ffn(x, g, wu, wd, tm, tf):
    rows, d = x.shape
    dff = wu.shape[1]
    return pl.pallas_call(
        _ffn_kernel,
        grid=(rows // tm, dff // tf),
        in_specs=[
            pl.BlockSpec((tm, d), lambda i, f: (i, 0)),
            pl.BlockSpec((1, d), lambda i, f: (0, 0)),
            pl.BlockSpec((d, tf), lambda i, f: (0, f)),
            pl.BlockSpec((tf, d), lambda i, f: (f, 0)),
        ],
        out_specs=pl.BlockSpec((tm, d), lambda i, f: (i, 0)),
        out_shape=jax.ShapeDtypeStruct((rows, d), F32),
        scratch_shapes=[pltpu.VMEM((tm, d), BF16)],
        compiler_params=_params("parallel", "arbitrary"),
        name="ffn",
    )(x, g, wu, wd)


def _qk_norm(x, g):
    lane = lax.broadcasted_iota(jnp.int32, x.shape, 1)
    lo = lane < HEAD_DIM_A
    x2 = x * x
    s_lo = jnp.sum(jnp.where(lo, x2, 0.0), axis=-1, keepdims=True)
    s_hi = jnp.sum(jnp.where(lo, 0.0, x2), axis=-1, keepdims=True)
    ms = jnp.where(lo, s_lo, s_hi) * (1.0 / HEAD_DIM_A)
    return x * lax.rsqrt(ms + EPS) * g


def _diff_lambda(l, lam_init):
    s01 = jnp.sum(l[0:1] * l[1:2], axis=-1, keepdims=True)
    s23 = jnp.sum(l[2:3] * l[3:4], axis=-1, keepdims=True)
    return jnp.exp(s01) - jnp.exp(s23) + lam_init


def _split3(x):
    b1 = x.astype(BF16).astype(F32)
    r1 = x - b1
    b2 = r1.astype(BF16).astype(F32)
    return b1, b2, r1 - b2


def _augment_k(kn, bias, half):
    lane = lax.broadcasted_iota(jnp.int32, kn.shape, 1)
    b1, b2, b3 = _split3(bias)
    base = HEAD_DIM_A if half == 0 else 0
    aug = jnp.where(lane == base, b1, jnp.where(lane == base + 1, b2, jnp.where(lane == base + 2, b3, 0.0)))
    keep = (lane < HEAD_DIM_A) if half == 0 else (lane >= HEAD_DIM_A)
    return jnp.where(keep, kn, aug).astype(BF16)


def _augment_q(qn, half):
    lane = lax.broadcasted_iota(jnp.int32, qn.shape, 1)
    base = HEAD_DIM_A if half == 0 else 0
    ones = jnp.where((lane >= base) & (lane < base + 3), 1.0, 0.0)
    keep = (lane < HEAD_DIM_A) if half == 0 else (lane >= HEAD_DIM_A)
    return jnp.where(keep, qn, ones).astype(BF16)


def _attn_kernel(q_ref, k_ref, v_ref, km_ref, vm_ref, qg_ref, kg_ref, ag_ref, lam_ref, sl_ref,
                 ko_ref, vo_ref, a_ref,
                 qa_scr, ka_scr, vb_scr, kma_scr, vmb_scr, m_scr, l_scr, acc_scr, *, lam_init, seq):
    hps = ATT_HEADS_PER_STEP
    tb = ATT_BLOCK
    nq = seq // tb
    qg = qg_ref[...]
    kg = kg_ref[...]
    lam = _diff_lambda(lam_ref[...], lam_init)

    row_s = lax.broadcasted_iota(jnp.int32, (seq, LANES), 0)
    row_m = lax.broadcasted_iota(jnp.int32, (N_META, LANES), 0)
    kma_scr[...] = jnp.zeros_like(kma_scr)
    vmb_scr[...] = jnp.zeros_like(vmb_scr)
    meta_valid = lax.broadcasted_iota(jnp.int32, (tb, LANES), 1) < N_META
    slopes2 = []
    for hh in range(hps):
        ln = slice(hh * LANES, (hh + 1) * LANES)
        slope2 = sl_ref[0, hh:hh + 1, :] * LOG2E
        slopes2.append(slope2)
        kn = _qk_norm(k_ref[:, ln], kg)
        kmn = _qk_norm(km_ref[:, ln], kg)
        ko_ref[0, 0:N_META, ln] = kmn
        ko_ref[0, N_META:, ln] = kn
        v = v_ref[:, ln]
        vmeta = vm_ref[:, ln]
        vo_ref[0, 0:N_META, ln] = vmeta
        vo_ref[0, N_META:, ln] = v
        vb_scr[hh] = v.astype(BF16)
        vmb_scr[hh, 0:N_META, :] = vmeta.astype(BF16)
        qn = _qk_norm(q_ref[:, ln], qg) * (HEAD_DIM_A ** -0.5 * LOG2E)
        bias = (row_s & (tb - 1)).astype(F32) * slope2
        bias_m = row_m.astype(F32) * slope2
        for mm in range(2):
            ka_scr[hh, mm] = _augment_k(kn, bias, mm)
            kma_scr[hh, mm, 0:N_META, :] = _augment_k(kmn, bias_m, mm)
            qa_scr[hh, mm] = _augment_q(qn, mm)

    def update(ch, s, c, pv, first):
        smax = jnp.max(s, axis=1, keepdims=True) + c
        if first:
            m_new = smax
        else:
            m_old = m_scr[ch][:, :1]
            m_new = jnp.maximum(m_old, smax)
        p = jnp.exp2(s - (m_new - c))
        psum = jnp.sum(p, axis=1, keepdims=True)
        contrib = pv(p.astype(BF16))
        if first:
            l_new = psum
            acc_new = contrib
        else:
            alpha = jnp.exp2(m_old - m_new)
            l_new = alpha * l_scr[ch][:, :1] + psum
            acc_new = alpha * acc_scr[ch] + contrib
        m_scr[ch] = jnp.broadcast_to(m_new, (tb, LANES))
        l_scr[ch] = jnp.broadcast_to(l_new, (tb, LANES))
        acc_scr[ch] = acc_new

    rel = (lax.broadcasted_iota(jnp.int32, (tb, tb), 1) - lax.broadcasted_iota(jnp.int32, (tb, tb), 0))

    for qi in range(nq):
        q0 = qi * tb
        for hh in range(hps):
            for mm in range(2):
                ch = hh * 2 + mm
                qa = qa_scr[hh, mm, q0:q0 + tb, :]
                s = jnp.where(meta_valid, _dot_nt(qa, kma_scr[hh, mm]), NEG_BIG)
                c = slopes2[hh][:, :1] * float(-N_META - q0)
                update(ch, s, c, lambda p, hh=hh: _dot(p, vmb_scr[hh]), True)

        def body(j, carry, q0=q0):
            k0 = pl.multiple_of(j * tb, tb)
            for hh in range(hps):
                for mm in range(2):
                    ch = hh * 2 + mm
                    qa = qa_scr[hh, mm, q0:q0 + tb, :]
                    s = _dot_nt(qa, ka_scr[hh, mm, pl.ds(k0, tb), :])
                    c = slopes2[hh][:, :1] * (k0 - q0).astype(F32)
                    update(ch, s, c, lambda p, hh=hh: _dot(p, vb_scr[hh, pl.ds(k0, tb), :]), False)
            return carry

        if qi > 0:
            lax.fori_loop(0, qi, body, 0)

        for hh in range(hps):
            for mm in range(2):
                ch = hh * 2 + mm
                qa = qa_scr[hh, mm, q0:q0 + tb, :]
                s = _dot_nt(qa, ka_scr[hh, mm, q0:q0 + tb, :])
                s = jnp.where(rel <= 0, s, NEG_BIG)
                update(ch, s, jnp.zeros((1, 1), F32), lambda p, hh=hh: _dot(p, vb_scr[hh, q0:q0 + tb, :]), False)

        for hh in range(hps):
            ln = slice(hh * LANES, (hh + 1) * LANES)
            o1 = acc_scr[2 * hh] / l_scr[2 * hh][:, :1]
            o2 = acc_scr[2 * hh + 1] / l_scr[2 * hh + 1][:, :1]
            a = o1 - lam * o2
            a = _rms(a, ag_ref[...]) * (1.0 - lam_init)
            a_ref[q0:q0 + tb, ln] = a.astype(BF16)


def _prompt_attention(z, z_s, qg2, kg2, ag, lambda_qk, slopes, lam_init, batch, seq):
    hps = ATT_HEADS_PER_STEP
    n_heads = slopes.shape[0]
    width = hps * LANES
    groups = n_heads // hps
    att_w = n_heads * V_DIM_A
    t_all = seq + N_META
    sl = jnp.broadcast_to(slopes.reshape(groups, hps, 1), (groups, hps, LANES))
    kernel = functools.partial(_attn_kernel, lam_init=lam_init, seq=seq)
    return pl.pallas_call(
        kernel,
        grid=(batch, groups),
        in_specs=[
            pl.BlockSpec((seq, width), lambda b, g: (b, g)),
            pl.BlockSpec((seq, width), lambda b, g: (b, groups + g)),
            pl.BlockSpec((seq, width), lambda b, g: (b, 2 * groups + g)),
            pl.BlockSpec((N_META, width), lambda b, g: (0, groups + g)),
            pl.BlockSpec((N_META, width), lambda b, g: (0, 2 * groups + g)),
            pl.BlockSpec((1, LANES), lambda b, g: (0, 0)),
            pl.BlockSpec((1, LANES), lambda b, g: (0, 0)),
            pl.BlockSpec((1, LANES), lambda b, g: (0, 0)),
            pl.BlockSpec(lambda_qk.shape, lambda b, g: (0, 0)),
            pl.BlockSpec((1, hps, LANES), lambda b, g: (g, 0, 0)),
        ],
        out_specs=[
            pl.BlockSpec((1, t_all, width), lambda b, g: (b, 0, g)),
            pl.BlockSpec((1, t_all, width), lambda b, g: (b, 0, g)),
            pl.BlockSpec((seq, width), lambda b, g: (b, g)),
        ],
        out_shape=[
            jax.ShapeDtypeStruct((batch, t_all, att_w), F32),
            jax.ShapeDtypeStruct((batch, t_all, att_w), F32),
            jax.ShapeDtypeStruct((batch * seq, att_w), BF16),
        ],
        scratch_shapes=[
            pltpu.VMEM((hps, 2, seq, LANES), BF16),
            pltpu.VMEM((hps, 2, seq, LANES), BF16),
            pltpu.VMEM((hps, seq, LANES), BF16),
            pltpu.VMEM((hps, 2, LANES, LANES), BF16),
            pltpu.VMEM((hps, LANES, LANES), BF16),
            pltpu.VMEM((2 * hps, ATT_BLOCK, LANES), F32),
            pltpu.VMEM((2 * hps, ATT_BLOCK, LANES), F32),
            pltpu.VMEM((2 * hps, ATT_BLOCK, LANES), F32),
        ],
        compiler_params=_params("parallel", "parallel"),
        name="prompt_attn",
    )(z, z, z, z_s, z_s, qg2, kg2, ag, lambda_qk, sl)


def _decode_kernel(pt_ref, *refs, lam_init, past_len, n_heads, dec_seq):
    del pt_ref
    pps = DEC_PAGES_PER_STEP
    k_refs = refs[:pps]
    v_refs = refs[pps:2 * pps]
    (zq_ref, zk_ref, zv_ref, qg_ref, kg_ref, ag_ref, lam_ref, slc_ref,
     a_ref, ks_ref, qm_scr, kn_scr, vn_scr, m_scr, l_scr, acc_scr) = refs[2 * pps:]
    st = pl.program_id(1)
    n_st = pl.num_programs(1)
    rows = 2 * n_heads * SAMPLE_PAD
    att_w = n_heads * V_DIM_A
    slope_c = slc_ref[:, :1] * LOG2E
    t_c = (lax.broadcasted_iota(jnp.int32, (rows, 1), 0) & (SAMPLE_PAD - 1)).astype(F32)

    @pl.when(st == 0)
    def _():
        qm_scr[...] = jnp.zeros_like(qm_scr)
        kn_scr[...] = jnp.zeros_like(kn_scr)
        vn_scr[...] = jnp.zeros_like(vn_scr)
        for h in range(n_heads):
            ln = slice(h * LANES, (h + 1) * LANES)
            qn = _qk_norm(zq_ref[:, ln], qg_ref[...]) * (HEAD_DIM_A ** -0.5 * LOG2E)
            lane = lax.broadcasted_iota(jnp.int32, qn.shape, 1)
            qm_scr[(2 * h) * SAMPLE_PAD:(2 * h + 1) * SAMPLE_PAD, ln] = jnp.where(lane < HEAD_DIM_A, qn, 0.0)
            qm_scr[(2 * h + 1) * SAMPLE_PAD:(2 * h + 2) * SAMPLE_PAD, ln] = jnp.where(lane < HEAD_DIM_A, 0.0, qn)
            kn = _qk_norm(zk_ref[:, ln], kg_ref[...])
            ks_ref[:, ln] = kn
            kn_scr[0:SAMPLE_PAD, ln] = kn
        vn_scr[0:SAMPLE_PAD, :] = zv_ref[...]
        m_scr[...] = jnp.full_like(m_scr, NEG_BIG)
        l_scr[...] = jnp.zeros_like(l_scr)
        acc_scr[...] = jnp.zeros_like(acc_scr)

    qm = qm_scr[...].astype(BF16)

    def update(s, c, pv):
        smax = jnp.max(s, axis=1, keepdims=True) + c
        m_old = m_scr[:, :1]
        m_new = jnp.maximum(m_old, smax)
        p = jnp.exp2(s - (m_new - c))
        alpha = jnp.exp2(m_old - m_new)
        l_scr[...] = jnp.broadcast_to(alpha * l_scr[:, :1] + jnp.sum(p, axis=1, keepdims=True), l_scr.shape)
        acc_scr[...] = alpha * acc_scr[...] + pv(p.astype(BF16))
        m_scr[...] = jnp.broadcast_to(m_new, m_scr.shape)

    k0 = (st * (pps * PAGE_SIZE)).astype(F32)
    s = jnp.concatenate([_dot_nt(qm, k_refs[p][0].astype(BF16)) for p in range(pps)], axis=1)
    kloc = lax.broadcasted_iota(jnp.int32, (1, pps * PAGE_SIZE), 1).astype(F32)
    s = s + slope_c * kloc
    c = slope_c * (k0 - (past_len + t_c))

    def pv_pages(p):
        out = _dot(p[:, 0:PAGE_SIZE], v_refs[0][0].astype(BF16))
        for i in range(1, pps):
            out = out + _dot(p[:, i * PAGE_SIZE:(i + 1) * PAGE_SIZE], v_refs[i][0].astype(BF16))
        return out

    update(s, c, pv_pages)

    @pl.when(st == n_st - 1)
    def _():
        sn = _dot_nt(qm, kn_scr[...].astype(BF16))
        kl = lax.broadcasted_iota(jnp.int32, (1, LANES), 1).astype(F32)
        sn = sn + slope_c * kl
        valid = (kl <= t_c) & (kl < dec_seq)
        sn = jnp.where(valid, sn, NEG_BIG)
        update(sn, slope_c * (0.0 - t_c), lambda p: _dot(p, vn_scr[...].astype(BF16)))

        lam = _diff_lambda(lam_ref[...], lam_init)
        for h in range(n_heads):
            ln = slice(h * LANES, (h + 1) * LANES)
            r1 = slice((2 * h) * SAMPLE_PAD, (2 * h + 1) * SAMPLE_PAD)
            r2 = slice((2 * h + 1) * SAMPLE_PAD, (2 * h + 2) * SAMPLE_PAD)
            o1 = acc_scr[r1, ln] / l_scr[r1, :1]
            o2 = acc_scr[r2, ln] / l_scr[r2, :1]
            a_ref[:, ln] = _rms(o1 - lam * o2, ag_ref[...]) * (1.0 - lam_init)


def _decode_attention(page_table, cache_k, cache_v, z_s, qg2, kg2, ag, lambda_qk, slopes, lam_init, dec_seq):
    pps = DEC_PAGES_PER_STEP
    dec_batch, n_pages = page_table.shape
    n_heads = slopes.shape[0]
    att_w = n_heads * V_DIM_A
    past_len = n_pages * PAGE_SIZE
    rows = 2 * n_heads * SAMPLE_PAD
    n_pool = cache_k.shape[0]
    ck = cache_k.reshape(n_pool, PAGE_SIZE, att_w)
    cv = cache_v.reshape(n_pool, PAGE_SIZE, att_w)
    slope_rows = jnp.broadcast_to(jnp.repeat(slopes, 2 * SAMPLE_PAD).reshape(rows, 1), (rows, LANES))
    meta_blocks = N_META // SAMPLE_PAD
    groups = att_w // att_w

    def page_spec(p):
        return pl.BlockSpec((1, PAGE_SIZE, att_w), lambda b, st, pt, p=p: (pt[b, st * pps + p], 0, 0))

    in_specs = [page_spec(p) for p in range(pps)] + [page_spec(p) for p in range(pps)] + [
        pl.BlockSpec((SAMPLE_PAD, att_w), lambda b, st, pt: (meta_blocks + b, 0)),
        pl.BlockSpec((SAMPLE_PAD, att_w), lambda b, st, pt: (meta_blocks + b, groups)),
        pl.BlockSpec((SAMPLE_PAD, att_w), lambda b, st, pt: (meta_blocks + b, 2 * groups)),
        pl.BlockSpec((1, LANES), lambda b, st, pt: (0, 0)),
        pl.BlockSpec((1, LANES), lambda b, st, pt: (0, 0)),
        pl.BlockSpec((1, LANES), lambda b, st, pt: (0, 0)),
        pl.BlockSpec(lambda_qk.shape, lambda b, st, pt: (0, 0)),
        pl.BlockSpec((rows, LANES), lambda b, st, pt: (0, 0)),
    ]
    kernel = functools.partial(_decode_kernel, lam_init=lam_init, past_len=float(past_len),
                               n_heads=n_heads, dec_seq=dec_seq)
    grid_spec = pltpu.PrefetchScalarGridSpec(
        num_scalar_prefetch=1,
        grid=(dec_batch, n_pages // pps),
        in_specs=in_specs,
        out_specs=[
            pl.BlockSpec((SAMPLE_PAD, att_w), lambda b, st, pt: (b, 0)),
            pl.BlockSpec((SAMPLE_PAD, att_w), lambda b, st, pt: (b, 0)),
        ],
        scratch_shapes=[
            pltpu.VMEM((rows, att_w), F32),
            pltpu.VMEM((PAGE_SIZE, att_w), F32),
            pltpu.VMEM((PAGE_SIZE, att_w), F32),
            pltpu.VMEM((rows, LANES), F32),
            pltpu.VMEM((rows, LANES), F32),
            pltpu.VMEM((rows, att_w), F32),
        ],
    )
    return pl.pallas_call(
        kernel,
        grid_spec=grid_spec,
        out_shape=[
            jax.ShapeDtypeStruct((dec_batch * SAMPLE_PAD, att_w), F32),
            jax.ShapeDtypeStruct((dec_batch * SAMPLE_PAD, att_w), F32),
        ],
        compiler_params=_params("parallel", "arbitrary"),
        name="decode_attn",
    )(page_table, *([ck] * pps), *([cv] * pps), z_s, z_s, z_s, qg2, kg2, ag, lambda_qk, slope_rows)


def _log_sigmoid(x):
    return jnp.minimum(x, 0.0) - jnp.log(1.0 + jnp.exp(-jnp.abs(x)))


def _gate_columns(g, gb, hm):
    lane = lax.broadcasted_iota(jnp.int32, g.shape, 1)
    gg = g + gb
    ig = jnp.sum(jnp.where(lane == hm, gg, 0.0), axis=-1, keepdims=True)
    fg = jnp.sum(jnp.where(lane == hm + N_HEADS_M, gg, 0.0), axis=-1, keepdims=True)
    return ig, _log_sigmoid(fg)


def _conv_silu(ubuf_ref, start, rows, cw, cb):
    acc = cb
    for j in range(CONV_W):
        acc = acc + ubuf_ref[start + 5 + j:start + 5 + j + rows, :] * cw[j:j + 1, :]
    return acc * jax.nn.sigmoid(acc)


def _mlstm_chunk(q, k, v, lf_c, ig_c, c0, n0, m0, want_h):
    L = q.shape[0]
    row = lax.broadcasted_iota(jnp.int32, (L, L), 0)
    col = lax.broadcasted_iota(jnp.int32, (L, L), 1)
    eye = row == col
    tril = col <= row
    lf_r = jnp.sum(jnp.where(eye, lf_c, 0.0), axis=0, keepdims=True)
    ig_r = jnp.sum(jnp.where(eye, ig_c, 0.0), axis=0, keepdims=True)
    b_c = jnp.sum(jnp.where(tril, lf_r, 0.0), axis=1, keepdims=True)
    b_r = jnp.sum(jnp.where(row <= col, lf_c, 0.0), axis=0, keepdims=True)
    d = jnp.where(tril, b_c - b_r + ig_r, NEG_BIG)
    inter = b_c + m0
    m_c = jnp.maximum(inter, jnp.max(d, axis=1, keepdims=True))
    b_l = b_c[L - 1:L]
    m_end = m_c[L - 1:L]
    w_c = jnp.exp(b_l + m0 - m_end)
    w_s = jnp.exp(b_l - b_c + ig_c - m_end)
    kw = k.astype(F32) * w_s
    c1 = w_c * c0 + _dot_tn(kw.astype(BF16), v)
    n1 = w_c * n0 + jnp.sum(kw, axis=0, keepdims=True)
    if not want_h:
        return c1, n1, m_end, None
    s = _dot_nt(q, k) * jnp.exp(d - m_c)
    w_i = jnp.exp(inter - m_c)
    num = w_i * _dot(q, c0.astype(BF16)) + _dot(s.astype(BF16), v)
    den = w_i * jnp.sum(q.astype(F32) * n0, axis=1, keepdims=True) + jnp.sum(s, axis=1, keepdims=True)
    h = num / jnp.maximum(jnp.abs(den), jnp.exp(-m_c))
    return c1, n1, m_end, h


def _mlstm_prompt_kernel(u_ref, vm_ref, og_ref, gt_ref, um_ref, vmm_ref, gtm_ref,
                         cw_ref, cb_ref, wq_ref, wk_ref, gb_ref, go_ref,
                         mo_ref, c_ref, n_ref, m_ref, cv_ref,
                         ubuf, q_scr, k_scr, v_scr, ig_scr, lf_scr, c_scr, n_scr, m_scr, *, seq):
    L = MLSTM_CHUNK
    hm = pl.program_id(1)
    d = u_ref.shape[1]
    n_chunks = seq // L
    cw = cw_ref[...]
    cb = cb_ref[...]
    wq = wq_ref[0].astype(BF16)
    wk = wk_ref[0].astype(BF16)
    kscale = d ** -0.5

    ubuf[0:SUBLANES, :] = jnp.zeros((SUBLANES, d), F32)
    ubuf[SUBLANES:SUBLANES + N_META, :] = um_ref[...]
    ubuf[SUBLANES + N_META:, :] = u_ref[...]

    def put(dst0, rows, ua, vrows, g):
        ub = ua.astype(BF16)
        q_scr[dst0:dst0 + rows, :] = _dot(ub, wq).astype(BF16)
        k_scr[dst0:dst0 + rows, :] = (_dot(ub, wk) * kscale).astype(BF16)
        v_scr[dst0:dst0 + rows, :] = vrows.astype(BF16)
        ig, lf = _gate_columns(g, gb_ref[...], hm)
        ig_scr[dst0:dst0 + rows, :] = ig
        lf_scr[dst0:dst0 + rows, :] = lf

    q_scr[0:L, :] = jnp.zeros((L, d), BF16)
    k_scr[0:L, :] = jnp.zeros((L, d), BF16)
    v_scr[0:L, :] = jnp.zeros((L, d), BF16)
    ig_scr[0:L, :] = jnp.full((L, 1), NEG_BIG, F32)
    lf_scr[0:L, :] = jnp.zeros((L, 1), F32)
    put(0, N_META, _conv_silu(ubuf, 0, N_META, cw, cb), vmm_ref[...], gtm_ref[...])
    piece = 2 * L
    for r0 in range(0, seq, piece):
        put(L + r0, piece, _conv_silu(ubuf, N_META + r0, piece, cw, cb),
            vm_ref[r0:r0 + piece, :], gt_ref[r0:r0 + piece, :])

    c1, n1, m1, _ = _mlstm_chunk(q_scr[0:L, :], k_scr[0:L, :], v_scr[0:L, :], lf_scr[0:L, :], ig_scr[0:L, :],
                                 jnp.zeros((d, d), F32), jnp.zeros((1, d), F32), jnp.zeros((1, 1), F32), False)
    c_scr[...] = c1
    n_scr[...] = n1
    m_scr[...] = jnp.broadcast_to(m1, m_scr.shape)

    def body(c, carry):
        r0 = pl.multiple_of((c + 1) * L, L)
        rs = pl.ds(r0, L)
        c1, n1, m1, h = _mlstm_chunk(q_scr[rs, :], k_scr[rs, :], v_scr[rs, :], lf_scr[rs, :], ig_scr[rs, :],
                                     c_scr[...], n_scr[...], m_scr[:, :1], True)
        c_scr[...] = c1
        n_scr[...] = n1
        m_scr[...] = jnp.broadcast_to(m1, m_scr.shape)
        os = pl.ds(pl.multiple_of(c * L, L), L)
        mo_ref[os, :] = (_rms(h, go_ref[...]) * jax.nn.sigmoid(og_ref[os, :])).astype(BF16)
        return carry

    lax.fori_loop(0, n_chunks, body, 0)
    c_ref[0] = c_scr[...]
    n_ref[0] = n_scr[...]
    m_ref[0] = m_scr[...]
    total = SUBLANES + N_META + seq
    cv_ref[0] = ubuf[total - (CONV_W - 1):total, :]


def _mlstm_prompt(z, gates, z_s, gates_s, conv_w, conv_b, w_qm, w_km, gbias, g_out, batch, seq, col0):
    d = w_qm.shape[1]
    nh = w_qm.shape[0]
    cb0 = col0 // d
    width = nh * d
    kernel = functools.partial(_mlstm_prompt_kernel, seq=seq)
    t_scr = MLSTM_CHUNK + seq
    return pl.pallas_call(
        kernel,
        grid=(batch, nh),
        in_specs=[
            pl.BlockSpec((seq, d), lambda b, h: (b, cb0 + h)),
            pl.BlockSpec((seq, d), lambda b, h: (b, cb0 + nh + h)),
            pl.BlockSpec((seq, d), lambda b, h: (b, cb0 + 2 * nh + h)),
            pl.BlockSpec((seq, LANES), lambda b, h: (b, 0)),
            pl.BlockSpec((N_META, d), lambda b, h: (0, cb0 + h)),
            pl.BlockSpec((N_META, d), lambda b, h: (0, cb0 + nh + h)),
            pl.BlockSpec((N_META, LANES), lambda b, h: (0, 0)),
            pl.BlockSpec((CONV_W, d), lambda b, h: (0, h)),
            pl.BlockSpec((1, d), lambda b, h: (0, h)),
            pl.BlockSpec((1, d, d), lambda b, h: (h, 0, 0)),
            pl.BlockSpec((1, d, d), lambda b, h: (h, 0, 0)),
            pl.BlockSpec((1, LANES), lambda b, h: (0, 0)),
            pl.BlockSpec((1, d), lambda b, h: (0, 0)),
        ],
        out_specs=[
            pl.BlockSpec((seq, d), lambda b, h: (b, h)),
            pl.BlockSpec((1, d, d), lambda b, h: (b * nh + h, 0, 0)),
            pl.BlockSpec((1, 1, d), lambda b, h: (b * nh + h, 0, 0)),
            pl.BlockSpec((1, 1, LANES), lambda b, h: (b * nh + h, 0, 0)),
            pl.BlockSpec((1, CONV_W - 1, d), lambda b, h: (b, 0, h)),
        ],
        out_shape=[
            jax.ShapeDtypeStruct((batch * seq, width), BF16),
            jax.ShapeDtypeStruct((batch * nh, d, d), F32),
            jax.ShapeDtypeStruct((batch * nh, 1, d), F32),
            jax.ShapeDtypeStruct((batch * nh, 1, LANES), F32),
            jax.ShapeDtypeStruct((batch, CONV_W - 1, width), F32),
        ],
        scratch_shapes=[
            pltpu.VMEM((SUBLANES + N_META + seq, d), F32),
            pltpu.VMEM((t_scr, d), BF16),
            pltpu.VMEM((t_scr, d), BF16),
            pltpu.VMEM((t_scr, d), BF16),
            pltpu.VMEM((t_scr, 1), F32),
            pltpu.VMEM((t_scr, 1), F32),
            pltpu.VMEM((d, d), F32),
            pltpu.VMEM((1, d), F32),
            pltpu.VMEM((1, LANES), F32),
        ],
        compiler_params=_params("parallel", "parallel"),
        name="mlstm_prompt",
    )(z, z, z, gates, z_s, z_s, gates_s, conv_w, conv_b, w_qm, w_km, gbias, g_out)


def _mlstm_sample_kernel(u_ref, vm_ref, og_ref, gt_ref, sc_ref, c0_ref, n0_ref, m0_ref,
                         cw_ref, cb_ref, wq_ref, wk_ref, gb_ref, go_ref,
                         mo_ref, c_ref, n_ref, m_ref, cv_ref,
                         ubuf, q_scr, k_scr, v_scr, ig_scr, lf_scr, *, dec_seq):
    L = MLSTM_CHUNK
    P = SAMPLE_PAD
    hm = pl.program_id(1)
    d = u_ref.shape[1]
    wq = wq_ref[0].astype(BF16)
    wk = wk_ref[0].astype(BF16)

    ubuf[0:SUBLANES, :] = jnp.zeros((SUBLANES, d), F32)
    ubuf[SUBLANES - (CONV_W - 1):SUBLANES, :] = sc_ref[0, 0]
    ubuf[SUBLANES:SUBLANES + P, :] = u_ref[...]
    ubuf[SUBLANES + P:, :] = jnp.zeros((SUBLANES, d), F32)

    ua = _conv_silu(ubuf, 0, P, cw_ref[...], cb_ref[...]).astype(BF16)
    q_scr[...] = jnp.zeros((L, d), BF16)
    k_scr[...] = jnp.zeros((L, d), BF16)
    v_scr[...] = jnp.zeros((L, d), BF16)
    q_scr[0:2 * P, :] = jnp.concatenate([_dot(ua, wq), jnp.zeros((P, d), F32)], axis=0).astype(BF16)
    k_scr[0:2 * P, :] = jnp.concatenate([_dot(ua, wk) * d ** -0.5, jnp.zeros((P, d), F32)], axis=0).astype(BF16)
    v_scr[0:2 * P, :] = jnp.concatenate([vm_ref[...], jnp.zeros((P, d), F32)], axis=0).astype(BF16)
    ig, lf = _gate_columns(gt_ref[...], gb_ref[...], hm)
    tok = lax.broadcasted_iota(jnp.int32, (P, 1), 0)
    ig_scr[...] = jnp.full((L, 1), NEG_BIG, F32)
    lf_scr[...] = jnp.zeros((L, 1), F32)
    ig_scr[0:P, :] = jnp.where(tok < dec_seq, ig, NEG_BIG)
    lf_scr[0:P, :] = jnp.where(tok < dec_seq, lf, 0.0)

    c1, n1, m1, h = _mlstm_chunk(q_scr[...], k_scr[...], v_scr[...], lf_scr[...], ig_scr[...],
                                 c0_ref[0], n0_ref[0], m0_ref[0][:, :1], True)
    c_ref[0] = c1
    n_ref[0] = n1
    m_ref[0] = jnp.broadcast_to(m1, (1, LANES))
    mo_ref[...] = _rms(h[0:P], go_ref[...]) * jax.nn.sigmoid(og_ref[...])
    last = SUBLANES + dec_seq
    cv_ref[0, 0] = ubuf[last - (CONV_W - 1):last, :]


def _mlstm_sample(z_s, gates_s, state_conv, c0, n0, m0, conv_w, conv_b, w_qm, w_km, gbias, g_out,
                  dec_batch, dec_seq, col0):
    d = w_qm.shape[1]
    nh = w_qm.shape[0]
    cb0 = col0 // d
    width = nh * d
    mb = N_META // SAMPLE_PAD
    L = MLSTM_CHUNK
    kernel = functools.partial(_mlstm_sample_kernel, dec_seq=dec_seq)
    return pl.pallas_call(
        kernel,
        grid=(dec_batch, nh),
        in_specs=[
            pl.BlockSpec((SAMPLE_PAD, d), lambda b, h: (mb + b, cb0 + h)),
            pl.BlockSpec((SAMPLE_PAD, d), lambda b, h: (mb + b, cb0 + nh + h)),
            pl.BlockSpec((SAMPLE_PAD, d), lambda b, h: (mb + b, cb0 + 2 * nh + h)),
            pl.BlockSpec((SAMPLE_PAD, LANES), lambda b, h: (mb + b, 0)),
            pl.BlockSpec((1, 1, CONV_W - 1, d), lambda b, h: (0, b, 0, h)),
            pl.BlockSpec((1, d, d), lambda b, h: (b * nh + h, 0, 0)),
            pl.BlockSpec((1, 1, d), lambda b, h: (b * nh + h, 0, 0)),
            pl.BlockSpec((1, 1, LANES), lambda b, h: (b * nh + h, 0, 0)),
            pl.BlockSpec((CONV_W, d), lambda b, h: (0, h)),
            pl.BlockSpec((1, d), lambda b, h: (0, h)),
            pl.BlockSpec((1, d, d), lambda b, h: (h, 0, 0)),
            pl.BlockSpec((1, d, d), lambda b, h: (h, 0, 0)),
            pl.BlockSpec((1, LANES), lambda b, h: (0, 0)),
            pl.BlockSpec((1, d), lambda b, h: (0, 0)),
        ],
        out_specs=[
            pl.BlockSpec((SAMPLE_PAD, d), lambda b, h: (b, h)),
            pl.BlockSpec((1, d, d), lambda b, h: (b * nh + h, 0, 0)),
            pl.BlockSpec((1, 1, d), lambda b, h: (b * nh + h, 0, 0)),
            pl.BlockSpec((1, 1, LANES), lambda b, h: (b * nh + h, 0, 0)),
            pl.BlockSpec((1, 1, CONV_W - 1, d), lambda b, h: (0, b, 0, h)),
        ],
        out_shape=[
            jax.ShapeDtypeStruct((dec_batch * SAMPLE_PAD, width), F32),
            jax.ShapeDtypeStruct((dec_batch * nh, d, d), F32),
            jax.ShapeDtypeStruct((dec_batch * nh, 1, d), F32),
            jax.ShapeDtypeStruct((dec_batch * nh, 1, LANES), F32),
            jax.ShapeDtypeStruct((1, dec_batch, CONV_W - 1, width), F32),
        ],
        scratch_shapes=[
            pltpu.VMEM((3 * SUBLANES, d), F32),
            pltpu.VMEM((L, d), BF16),
            pltpu.VMEM((L, d), BF16),
            pltpu.VMEM((L, d), BF16),
            pltpu.VMEM((L, 1), F32),
            pltpu.VMEM((L, 1), F32),
        ],
        compiler_params=_params("parallel", "parallel"),
        name="mlstm_sample",
    )(z_s, z_s, z_s, gates_s, state_conv, c0, n0, m0, conv_w, conv_b, w_qm, w_km, gbias, g_out)


def _layer(l, x_p, x_s, cache_k, cache_v, state_C, state_n, state_m, state_conv, page_table,
           norm_mix, w_in, q_norm, k_norm, lambda_qk, attn_out_norm, conv_w, conv_b, w_qm, w_km,
           gate_bias, mlstm_out_norm, w_out, norm_ffn, w_up, w_down, *, batch, seq, dec_batch, dec_seq):
    d_model = x_p.shape[1]
    nh_m = w_qm.shape[0]
    d_m = w_qm.shape[1]
    mlstm_w = nh_m * d_m
    att_w = d_model - mlstm_w
    n_heads_a = att_w // V_DIM_A
    main_w = 3 * att_w + 3 * mlstm_w
    lam_init = 0.8 - 0.6 * math.exp(-0.3 * l)
    slopes = 2.0 ** (-8.0 * jnp.arange(1, n_heads_a + 1, dtype=F32) / n_heads_a)

    w_main = w_in[:, :main_w].astype(BF16)
    wg = jnp.pad(w_in[:, main_w:], ((0, 0), (0, LANES - 2 * nh_m)))
    wgh = wg.astype(BF16)
    wgl = (wg - wgh.astype(F32)).astype(BF16)
    g_mix = norm_mix.reshape(1, d_model)
    qg2 = jnp.tile(q_norm, 2).reshape(1, V_DIM_A)
    kg2 = jnp.tile(k_norm, 2).reshape(1, V_DIM_A)
    ag = attn_out_norm.reshape(1, V_DIM_A)
    gbias = jnp.pad(gate_bias.reshape(1, 2 * nh_m), ((0, 0), (0, LANES - 2 * nh_m)))
    cb = conv_b.reshape(1, mlstm_w)
    g_out = mlstm_out_norm.reshape(1, d_m)
    wo_a = w_out[:att_w].astype(BF16)
    wo_m = w_out[att_w:].astype(BF16)
    g_ffn = norm_ffn.reshape(1, d_model)
    wu = w_up.astype(BF16)
    wd = w_down.astype(BF16)

    rows_s = x_s.shape[0]
    z, gates = _inproj(x_p, g_mix, w_main, wgh, wgl, tm=1024, tn=512)
    z_s, gates_s = _inproj(x_s, g_mix, w_main, wgh, wgl, tm=rows_s, tn=512)

    k_p, v_p, a_p = _prompt_attention(z, z_s, qg2, kg2, ag, lambda_qk, slopes, lam_init, batch, seq)
    a_s, ks = _decode_attention(page_table, cache_k, cache_v, z_s, qg2, kg2, ag, lambda_qk, slopes,
                                lam_init, dec_seq)

    m_p, c_p, n_p, mm_p, cv_p = _mlstm_prompt(z, gates, z_s, gates_s, conv_w, cb, w_qm, w_km, gbias, g_out,
                                              batch, seq, 3 * att_w)
    c0 = state_C.reshape(dec_batch * nh_m, d_m, d_m)
    n0 = state_n.reshape(dec_batch * nh_m, 1, d_m)
    m0 = jnp.broadcast_to(state_m.reshape(dec_batch * nh_m, 1, 1), (dec_batch * nh_m, 1, LANES))
    m_s, c_s, n_s, mm_s, cv_s = _mlstm_sample(z_s, gates_s, state_conv, c0, n0, m0, conv_w, cb, w_qm, w_km,
                                              gbias, g_out, dec_batch, dec_seq, 3 * att_w)

    x_sample_rows = x_s[N_META:]
    x1_p = _outproj(x_p, a_p, m_p, wo_a, wo_m, tm=1024, tn=512)
    x1_s = _outproj(x_sample_rows, a_s, m_s, wo_a, wo_m, tm=x_sample_rows.shape[0], tn=512)
    y_p = _ffn(x1_p, g_ffn, wu, wd, tm=1024, tf=512)
    y_s = _ffn(x1_s, g_ffn, wu, wd, tm=x1_s.shape[0], tf=512)

    def unpad(a):
        return a.reshape(dec_batch, SAMPLE_PAD, a.shape[-1])[:, :dec_seq]

    outs = dict(
        y_p=y_p, y_s=unpad(y_s),
        k_p=k_p.reshape(batch, seq + N_META, n_heads_a, V_DIM_A),
        v_p=v_p.reshape(batch, seq + N_META, n_heads_a, V_DIM_A),
        c_p=c_p.reshape(batch, nh_m, d_m, d_m), n_p=n_p.reshape(batch, nh_m, d_m),
        m_p=mm_p[:, 0, 0].reshape(batch, nh_m), cv_p=cv_p,
        k_s=unpad(ks).reshape(dec_batch, dec_seq, n_heads_a, V_DIM_A),
        v_s=unpad(z_s[N_META:, 2 * att_w:3 * att_w]).reshape(dec_batch, dec_seq, n_heads_a, V_DIM_A),
        c_s=c_s.reshape(dec_batch, nh_m, d_m, d_m), n_s=n_s.reshape(dec_batch, nh_m, d_m),
        m_s=mm_s[:, 0, 0].reshape(dec_batch, nh_m), cv_s=cv_s[0],
    )
    return outs


def kernel(x_prompt, x_sample, cache_k, cache_v, state_C, state_n, state_m, state_conv, page_table, meta_tokens, norm_mix, w_in, q_norm, k_norm, lambda_qk, attn_out_norm, conv_w, conv_b, w_qm, w_km, gate_bias, mlstm_out_norm, w_out, norm_ffn, w_up, w_down):
    batch, seq, d_model = x_prompt.shape
    dec_batch, dec_seq, _ = x_sample.shape
    depth = w_in.shape[0]
    assert depth == 1, "prompt rows are kept without the meta tokens, which is only valid for one layer"
    assert dec_seq <= SAMPLE_PAD and seq % ATT_BLOCK == 0 and seq % (2 * MLSTM_CHUNK) == 0

    x_p = x_prompt.reshape(batch * seq, d_model)
    xs_pad = jnp.pad(x_sample, ((0, 0), (0, SAMPLE_PAD - dec_seq), (0, 0))).reshape(dec_batch * SAMPLE_PAD, d_model)
    x_s = jnp.concatenate([meta_tokens.astype(x_prompt.dtype), xs_pad], axis=0)

    l = 0
    o = _layer(l, x_p, x_s, cache_k[l], cache_v[l], state_C[l], state_n[l], state_m[l], state_conv[l:l + 1],
               page_table, norm_mix[l], w_in[l], q_norm[l], k_norm[l], lambda_qk[l], attn_out_norm[l],
               conv_w[l], conv_b[l], w_qm[l], w_km[l], gate_bias[l], mlstm_out_norm[l], w_out[l],
               norm_ffn[l], w_up[l], w_down[l], batch=batch, seq=seq, dec_batch=dec_batch, dec_seq=dec_seq)
    st = lambda a: a[None]
    return (o["y_p"].reshape(batch, seq, d_model), o["y_s"],
            st(o["k_p"]), st(o["v_p"]), st(o["c_p"]), st(o["n_p"]), st(o["m_p"]), st(o["cv_p"]),
            st(o["k_s"]), st(o["v_s"]), st(o["c_s"]), st(o["n_s"]), st(o["m_s"]), st(o["cv_s"]))
```

```python
import functools
import math

import jax
import jax.numpy as jnp
from jax import lax
from jax.experimental import pallas as pl
from jax.experimental.pallas import tpu as pltpu

F32 = jnp.float32
BF16 = jnp.bfloat16

N_META = 16
HEAD_DIM_A = 64
V_DIM_A = 2 * HEAD_DIM_A
N_HEADS_M = 4
CONV_W = 4
PAGE_SIZE = 128
EPS = 1e-6

LANES = 128
SUBLANES = 8
VMEM_LIMIT_BYTES = 56 * 1024 * 1024

LOG2E = 1.4426950408889634
NEG_BIG = -1e30

SAMPLE_PAD = SUBLANES
MLSTM_CHUNK = 256
ATT_QBLOCK = 512
ATT_KBLOCK = 256
ATT_HEADS_PER_STEP = 2
DEC_PAGES_PER_STEP = 8


def _dot(a, b):
    return jnp.dot(a, b, preferred_element_type=F32)


def _dot_nt(a, b):
    return lax.dot_general(a, b, (((1,), (1,)), ((), ())), preferred_element_type=F32)


def _dot_tn(a, b):
    return lax.dot_general(a, b, (((0,), (0,)), ((), ())), preferred_element_type=F32)


def _rms(x, g):
    return x * lax.rsqrt(jnp.mean(x * x, axis=-1, keepdims=True) + EPS) * g


def _params(*sem):
    return pltpu.CompilerParams(dimension_semantics=sem, vmem_limit_bytes=VMEM_LIMIT_BYTES)


def _inproj_kernel(x_ref, g_ref, w_ref, wgh_ref, wgl_ref, z_ref, gate_ref, xn_scr):
    @pl.when(pl.program_id(1) == 0)
    def _():
        xn = _rms(x_ref[...], g_ref[...])
        xh = xn.astype(BF16)
        xl = (xn - xh.astype(F32)).astype(BF16)
        xn_scr[...] = xh
        wgh = wgh_ref[...]
        gate_ref[...] = _dot(xh, wgh) + (_dot(xl, wgh) + _dot(xh, wgl_ref[...]))

    z_ref[...] = _dot(xn_scr[...], w_ref[...])


def _inproj(x, g, w, wgh, wgl, tm, tn):
    rows, d = x.shape
    n = w.shape[1]
    return pl.pallas_call(
        _inproj_kernel,
        grid=(rows // tm, n // tn),
        in_specs=[
            pl.BlockSpec((tm, d), lambda i, j: (i, 0)),
            pl.BlockSpec((1, d), lambda i, j: (0, 0)),
            pl.BlockSpec((d, tn), lambda i, j: (0, j)),
            pl.BlockSpec((d, LANES), lambda i, j: (0, 0)),
            pl.BlockSpec((d, LANES), lambda i, j: (0, 0)),
        ],
        out_specs=[
            pl.BlockSpec((tm, tn), lambda i, j: (i, j)),
            pl.BlockSpec((tm, LANES), lambda i, j: (i, 0)),
        ],
        out_shape=[jax.ShapeDtypeStruct((rows, n), F32), jax.ShapeDtypeStruct((rows, LANES), F32)],
        scratch_shapes=[pltpu.VMEM((tm, d), BF16)],
        compiler_params=_params("parallel", "arbitrary"),
        name="inproj",
    )(x, g, w, wgh, wgl)


def _outproj_kernel(x_ref, a_ref, m_ref, wa_ref, wm_ref, o_ref):
    a = a_ref[...].astype(BF16)
    m = m_ref[...].astype(BF16)
    o_ref[...] = x_ref[...] + (_dot(a, wa_ref[...]) + _dot(m, wm_ref[...]))


def _outproj(x, a, m, wa, wm, tm, tn):
    rows, d = x.shape
    ka = a.shape[1]
    km = m.shape[1]
    return pl.pallas_call(
        _outproj_kernel,
        grid=(rows // tm, d // tn),
        in_specs=[
            pl.BlockSpec((tm, tn), lambda i, j: (i, j)),
            pl.BlockSpec((tm, ka), lambda i, j: (i, 0)),
            pl.BlockSpec((tm, km), lambda i, j: (i, 0)),
            pl.BlockSpec((ka, tn), lambda i, j: (0, j)),
            pl.BlockSpec((km, tn), lambda i, j: (0, j)),
        ],
        out_specs=pl.BlockSpec((tm, tn), lambda i, j: (i, j)),
        out_shape=jax.ShapeDtypeStruct((rows, d), F32),
        compiler_params=_params("parallel", "arbitrary"),
        name="outproj",
    )(x, a, m, wa, wm)


def _ffn_kernel(x_ref, g_ref, wu_ref, wd_ref, o_ref, xn_scr):
    @pl.when(pl.program_id(1) == 0)
    def _():
        x = x_ref[...]
        xn_scr[...] = _rms(x, g_ref[...]).astype(BF16)
        o_ref[...] = x

    h = _dot(xn_scr[...], wu_ref[...])
    h = jnp.square(jnp.maximum(h, 0.0))
    o_ref[...] += _dot(h.astype(BF16), wd_ref[...])


def _ffn(x, g, wu, wd, tm, tf):
    rows, d = x.shape
    dff = wu.shape[1]
    return pl.pallas_call(
        _ffn_kernel,
        grid=(rows // tm, dff // tf),
        in_specs=[
            pl.BlockSpec((tm, d), lambda i, f: (i, 0)),
            pl.BlockSpec((1, d), lambda i, f: (0, 0)),
            pl.BlockSpec((d, tf), lambda i, f: (0, f)),
            pl.BlockSpec((tf, d), lambda i, f: (f, 0)),
        ],
        out_specs=pl.BlockSpec((tm, d), lambda i, f: (i, 0)),
        out_shape=jax.ShapeDtypeStruct((rows, d), F32),
        scratch_shapes=[pltpu.VMEM((tm, d), BF16)],
        compiler_params=_params("parallel", "arbitrary"),
        name="ffn",
    )(x, g, wu, wd)


def _qk_norm(x, g):
    lane = lax.broadcasted_iota(jnp.int32, x.shape, 1)
    lo = lane < HEAD_DIM_A
    x2 = x * x
    s_lo = jnp.sum(jnp.where(lo, x2, 0.0), axis=-1, keepdims=True)
    s_hi = jnp.sum(jnp.where(lo, 0.0, x2), axis=-1, keepdims=True)
    ms = jnp.where(lo, s_lo, s_hi) * (1.0 / HEAD_DIM_A)
    return x * lax.rsqrt(ms + EPS) * g


def _diff_lambda(l, lam_init):
    s01 = jnp.sum(l[0:1] * l[1:2], axis=-1, keepdims=True)
    s23 = jnp.sum(l[2:3] * l[3:4], axis=-1, keepdims=True)
    return jnp.exp(s01) - jnp.exp(s23) + lam_init


def _split3(x):
    b1 = x.astype(BF16).astype(F32)
    r1 = x - b1
    b2 = r1.astype(BF16).astype(F32)
    return b1, b2, r1 - b2


def _augment_k(kn, bias, half):
    lane = lax.broadcasted_iota(jnp.int32, kn.shape, 1)
    b1, b2, b3 = _split3(bias)
    base = HEAD_DIM_A if half == 0 else 0
    aug = jnp.where(lane == base, b1, jnp.where(lane == base + 1, b2, jnp.where(lane == base + 2, b3, 0.0)))
    keep = (lane < HEAD_DIM_A) if half == 0 else (lane >= HEAD_DIM_A)
    return jnp.where(keep, kn, aug).astype(BF16)


def _augment_q(qn, half):
    lane = lax.broadcasted_iota(jnp.int32, qn.shape, 1)
    base = HEAD_DIM_A if half == 0 else 0
    ones = jnp.where((lane >= base) & (lane < base + 3), 1.0, 0.0)
    keep = (lane < HEAD_DIM_A) if half == 0 else (lane >= HEAD_DIM_A)
    return jnp.where(keep, qn, ones)


def _attn_kernel(q_ref, k_ref, v_ref, km_ref, vm_ref, qg_ref, kg_ref, ag_ref, lam_ref, sl_ref,
                 ko_ref, vo_ref, a_ref,
                 qt_scr, ka_scr, vt_scr, kma_scr, vtm_scr, m_scr, l_scr, ot_scr, *, lam_init, seq):
    hps = ATT_HEADS_PER_STEP
    tq = ATT_QBLOCK
    tk = ATT_KBLOCK
    qg = qg_ref[...]
    kg = kg_ref[...]
    lam = _diff_lambda(lam_ref[...], lam_init)

    row_s = lax.broadcasted_iota(jnp.int32, (seq, LANES), 0)
    row_m = lax.broadcasted_iota(jnp.int32, (N_META, LANES), 0)
    kma_scr[...] = jnp.zeros_like(kma_scr)
    slopes2 = []
    for hh in range(hps):
        ln = slice(hh * LANES, (hh + 1) * LANES)
        slope2 = sl_ref[0, hh:hh + 1, :] * LOG2E
        slopes2.append(slope2[:, :1])
        kn = _qk_norm(k_ref[:, ln], kg)
        kmn = _qk_norm(km_ref[:, ln], kg)
        ko_ref[0, 0:N_META, ln] = kmn
        ko_ref[0, N_META:, ln] = kn
        v = v_ref[:, ln]
        vmeta = vm_ref[:, ln]
        vo_ref[0, 0:N_META, ln] = vmeta
        vo_ref[0, N_META:, ln] = v
        for kb in range(seq // tk):
            vt_scr[hh, kb] = v[kb * tk:(kb + 1) * tk, :].T.astype(BF16)
        vpad = jnp.concatenate([vmeta, jnp.zeros((LANES - N_META, LANES), F32)], axis=0)
        vtm_scr[hh] = vpad.T.astype(BF16)
        qn = _qk_norm(q_ref[:, ln], qg) * (HEAD_DIM_A ** -0.5 * LOG2E)
        bias = (row_s & (tk - 1)).astype(F32) * slope2
        bias_m = row_m.astype(F32) * slope2
        for mm in range(2):
            ka_scr[hh, mm] = _augment_k(kn, bias, mm)
            kma_scr[hh, mm, 0:N_META, :] = _augment_k(kmn, bias_m, mm)
            qa = _augment_q(qn, mm)
            for qb in range(seq // tq):
                qt_scr[hh, mm, qb] = qa[qb * tq:(qb + 1) * tq, :].T.astype(BF16)

    rel = (lax.broadcasted_iota(jnp.int32, (tk, tq), 0) - lax.broadcasted_iota(jnp.int32, (tk, tq), 1))
    meta_valid = lax.broadcasted_iota(jnp.int32, (LANES, tq), 0) < N_META
    chains = [(hh, mm) for hh in range(hps) for mm in range(2)]
    kpq = tq // tk

    def q_block(qi, carry):
        q0 = pl.multiple_of(qi * tq, tq)

        sds = [[_dot(ka_scr[hh, mm, pl.ds(q0 + d * tk, tk), :], qt_scr[hh, mm, qi]) for d in range(kpq)]
               for hh, mm in chains]
        sms = [_dot(kma_scr[hh, mm], qt_scr[hh, mm, qi]) for hh, mm in chains]
        pds, pms = [], []
        for (hh, mm), sd, sm in zip(chains, sds, sms):
            ch = hh * 2 + mm
            sd = [jnp.where(rel <= -d * tk, s + slopes2[hh] * float(d * tk), NEG_BIG) for d, s in enumerate(sd)]
            cm = slopes2[hh] * (-N_META - q0).astype(F32)
            sm = jnp.where(meta_valid, sm + cm, NEG_BIG)
            m = jnp.max(sm, axis=0, keepdims=True)
            for s in sd:
                m = jnp.maximum(m, jnp.max(s, axis=0, keepdims=True))
            pm = jnp.exp2(sm - m)
            l = jnp.sum(pm, axis=0, keepdims=True)
            pd = [jnp.exp2(s - m) for s in sd]
            for p in pd:
                l = l + jnp.sum(p, axis=0, keepdims=True)
            m_scr[ch] = jnp.broadcast_to(m, (SUBLANES, tq))
            l_scr[ch] = jnp.broadcast_to(l, (SUBLANES, tq))
            pds.append([p.astype(BF16) for p in pd])
            pms.append(pm.astype(BF16))
        for (hh, mm), pd, pm in zip(chains, pds, pms):
            o = _dot(vtm_scr[hh], pm)
            for d, p in enumerate(pd):
                o = o + _dot(vt_scr[hh, qi * kpq + d], p)
            ot_scr[hh * 2 + mm] = o

        def k_block(j, carry):
            k0 = pl.multiple_of(j * tk, tk)
            ss = [_dot(ka_scr[hh, mm, pl.ds(k0, tk), :], qt_scr[hh, mm, qi]) for hh, mm in chains]
            ps, alphas = [], []
            for (hh, mm), s in zip(chains, ss):
                ch = hh * 2 + mm
                c = slopes2[hh] * (k0 - q0).astype(F32)
                m_old = m_scr[ch][0:1]
                m_new = jnp.maximum(m_old, jnp.max(s, axis=0, keepdims=True) + c)
                p = jnp.exp2(s - (m_new - c))
                alpha = jnp.exp2(m_old - m_new)
                l = alpha * l_scr[ch][0:1] + jnp.sum(p, axis=0, keepdims=True)
                m_scr[ch] = jnp.broadcast_to(m_new, (SUBLANES, tq))
                l_scr[ch] = jnp.broadcast_to(l, (SUBLANES, tq))
                ps.append(p.astype(BF16))
                alphas.append(alpha)
            for (hh, mm), p, alpha in zip(chains, ps, alphas):
                ch = hh * 2 + mm
                ot_scr[ch] = alpha * ot_scr[ch] + _dot(vt_scr[hh, j], p)
            return carry

        lax.fori_loop(0, qi * kpq, k_block, 0)

        for hh in range(hps):
            ln = slice(hh * LANES, (hh + 1) * LANES)
            o1 = ot_scr[2 * hh] / l_scr[2 * hh][0:1]
            o2 = ot_scr[2 * hh + 1] / l_scr[2 * hh + 1][0:1]
            a = (o1 - lam * o2).T
            a = _rms(a, ag_ref[...]) * (1.0 - lam_init)
            a_ref[pl.ds(q0, tq), ln] = a.astype(BF16)
        return carry

    lax.fori_loop(0, seq // tq, q_block, 0)


def _prompt_attention(z, z_s, qg2, kg2, ag, lambda_qk, slopes, lam_init, batch, seq):
    hps = ATT_HEADS_PER_STEP
    n_heads = slopes.shape[0]
    width = hps * LANES
    groups = n_heads // hps
    att_w = n_heads * V_DIM_A
    t_all = seq + N_META
    sl = jnp.broadcast_to(slopes.reshape(groups, hps, 1), (groups, hps, LANES))
    kernel = functools.partial(_attn_kernel, lam_init=lam_init, seq=seq)
    return pl.pallas_call(
        kernel,
        grid=(batch, groups),
        in_specs=[
            pl.BlockSpec((seq, width), lambda b, g: (b, g)),
            pl.BlockSpec((seq, width), lambda b, g: (b, groups + g)),
            pl.BlockSpec((seq, width), lambda b, g: (b, 2 * groups + g)),
            pl.BlockSpec((N_META, width), lambda b, g: (0, groups + g)),
            pl.BlockSpec((N_META, width), lambda b, g: (0, 2 * groups + g)),
            pl.BlockSpec((1, LANES), lambda b, g: (0, 0)),
            pl.BlockSpec((1, LANES), lambda b, g: (0, 0)),
            pl.BlockSpec((1, LANES), lambda b, g: (0, 0)),
            pl.BlockSpec(lambda_qk.shape, lambda b, g: (0, 0)),
            pl.BlockSpec((1, hps, LANES), lambda b, g: (g, 0, 0)),
        ],
        out_specs=[
            pl.BlockSpec((1, t_all, width), lambda b, g: (b, 0, g)),
            pl.BlockSpec((1, t_all, width), lambda b, g: (b, 0, g)),
            pl.BlockSpec((seq, width), lambda b, g: (b, g)),
        ],
        out_shape=[
            jax.ShapeDtypeStruct((batch, t_all, att_w), F32),
            jax.ShapeDtypeStruct((batch, t_all, att_w), F32),
            jax.ShapeDtypeStruct((batch * seq, att_w), BF16),
        ],
        scratch_shapes=[
            pltpu.VMEM((hps, 2, seq // ATT_QBLOCK, LANES, ATT_QBLOCK), BF16),
            pltpu.VMEM((hps, 2, seq, LANES), BF16),
            pltpu.VMEM((hps, seq // ATT_KBLOCK, LANES, ATT_KBLOCK), BF16),
            pltpu.VMEM((hps, 2, LANES, LANES), BF16),
            pltpu.VMEM((hps, LANES, LANES), BF16),
            pltpu.VMEM((2 * hps, SUBLANES, ATT_QBLOCK), F32),
            pltpu.VMEM((2 * hps, SUBLANES, ATT_QBLOCK), F32),
            pltpu.VMEM((2 * hps, LANES, ATT_QBLOCK), F32),
        ],
        compiler_params=_params("parallel", "parallel"),
        name="prompt_attn",
    )(z, z, z, z_s, z_s, qg2, kg2, ag, lambda_qk, sl)


def _decode_kernel(pt_ref, *refs, lam_init, past_len, n_heads, dec_seq):
    del pt_ref
    pps = DEC_PAGES_PER_STEP
    k_refs = refs[:pps]
    v_refs = refs[pps:2 * pps]
    (zq_ref, zk_ref, zv_ref, qg_ref, kg_ref, ag_ref, lam_ref, slc_ref,
     a_ref, ks_ref, qall_scr, knf_scr, vnf_scr, mb_scr, m_scr, l_scr, acc_scr) = refs[2 * pps:]
    st = pl.program_id(1)
    n_st = pl.num_programs(1)
    rows = 2 * n_heads * SAMPLE_PAD
    page_rows = PAGE_SIZE * n_heads
    width = pps * page_rows
    hshift = n_heads.bit_length() - 1
    slope_c = slc_ref[:, :1] * LOG2E
    row_i = lax.broadcasted_iota(jnp.int32, (rows, 1), 0)
    t_c = (row_i & (SAMPLE_PAD - 1)).astype(F32)
    h_c = row_i >> (hshift + 1)

    @pl.when(st == 0)
    def _():
        qall_scr[...] = jnp.zeros_like(qall_scr)
        knf_scr[...] = jnp.zeros_like(knf_scr)
        vnf_scr[...] = jnp.zeros_like(vnf_scr)
        for h in range(n_heads):
            ln = slice(h * LANES, (h + 1) * LANES)
            qn = _qk_norm(zq_ref[:, ln], qg_ref[...]) * (HEAD_DIM_A ** -0.5 * LOG2E)
            lane = lax.broadcasted_iota(jnp.int32, qn.shape, 1)
            qall_scr[(2 * h) * SAMPLE_PAD:(2 * h + 1) * SAMPLE_PAD, :] = jnp.where(lane < HEAD_DIM_A, qn, 0.0)
            qall_scr[(2 * h + 1) * SAMPLE_PAD:(2 * h + 2) * SAMPLE_PAD, :] = jnp.where(lane < HEAD_DIM_A, 0.0, qn)
            kn = _qk_norm(zk_ref[:, ln], kg_ref[...])
            ks_ref[:, ln] = kn
            knf_scr[h * SAMPLE_PAD:(h + 1) * SAMPLE_PAD, :] = kn
            vnf_scr[h * SAMPLE_PAD:(h + 1) * SAMPLE_PAD, :] = zv_ref[:, ln]
        lane = lax.broadcasted_iota(jnp.int32, (rows, width), 1)
        tok = (lane >> hshift).astype(F32)
        mb_scr[...] = jnp.where((lane & (n_heads - 1)) == h_c, slope_c * tok, NEG_BIG)
        m_scr[...] = jnp.full_like(m_scr, NEG_BIG)
        l_scr[...] = jnp.zeros_like(l_scr)
        acc_scr[...] = jnp.zeros_like(acc_scr)

    qall = qall_scr[...].astype(BF16)

    def update(s, c, pv):
        smax = jnp.max(s, axis=1, keepdims=True) + c
        m_old = m_scr[:, :1]
        m_new = jnp.maximum(m_old, smax)
        p = jnp.exp2(s - (m_new - c))
        alpha = jnp.exp2(m_old - m_new)
        l_scr[...] = jnp.broadcast_to(alpha * l_scr[:, :1] + jnp.sum(p, axis=1, keepdims=True), l_scr.shape)
        acc_scr[...] = alpha * acc_scr[...] + pv(p.astype(BF16))
        m_scr[...] = jnp.broadcast_to(m_new, m_scr.shape)

    def flat(ref):
        return ref[0, 0].reshape(page_rows, V_DIM_A).astype(BF16)

    k0 = (st * (pps * PAGE_SIZE)).astype(F32)
    s = jnp.concatenate([_dot_nt(qall, flat(k_refs[p])) for p in range(pps)], axis=1) + mb_scr[...]
    c = slope_c * (k0 - (past_len + t_c))

    def pv_pages(p):
        out = _dot(p[:, 0:page_rows], flat(v_refs[0]))
        for i in range(1, pps):
            out = out + _dot(p[:, i * page_rows:(i + 1) * page_rows], flat(v_refs[i]))
        return out

    update(s, c, pv_pages)

    @pl.when(st == n_st - 1)
    def _():
        sn = _dot_nt(qall, knf_scr[...].astype(BF16))
        lane = lax.broadcasted_iota(jnp.int32, (rows, LANES), 1)
        tk = lane & (SAMPLE_PAD - 1)
        tkf = tk.astype(F32)
        valid = ((lane >> 3) == h_c) & (tkf <= t_c) & (tk < dec_seq)
        sn = jnp.where(valid, sn + slope_c * tkf, NEG_BIG)
        update(sn, slope_c * (0.0 - t_c), lambda p: _dot(p, vnf_scr[...].astype(BF16)))

        lam = _diff_lambda(lam_ref[...], lam_init)
        for h in range(n_heads):
            ln = slice(h * LANES, (h + 1) * LANES)
            r1 = slice((2 * h) * SAMPLE_PAD, (2 * h + 1) * SAMPLE_PAD)
            r2 = slice((2 * h + 1) * SAMPLE_PAD, (2 * h + 2) * SAMPLE_PAD)
            o1 = acc_scr[r1, :] / l_scr[r1, :1]
            o2 = acc_scr[r2, :] / l_scr[r2, :1]
            a_ref[:, ln] = _rms(o1 - lam * o2, ag_ref[...]) * (1.0 - lam_init)


def _decode_attention(page_table, cache_k, cache_v, layer, z_s, qg2, kg2, ag, lambda_qk, slopes, lam_init, dec_seq):
    pps = DEC_PAGES_PER_STEP
    dec_batch, n_pages = page_table.shape
    n_heads = slopes.shape[0]
    assert n_heads == SAMPLE_PAD and n_heads & (n_heads - 1) == 0
    att_w = n_heads * V_DIM_A
    past_len = n_pages * PAGE_SIZE
    rows = 2 * n_heads * SAMPLE_PAD
    page_rows = PAGE_SIZE * n_heads
    slope_rows = jnp.broadcast_to(jnp.repeat(slopes, 2 * SAMPLE_PAD).reshape(rows, 1), (rows, LANES))
    meta_blocks = N_META // SAMPLE_PAD

    def page_spec(p):
        return pl.BlockSpec((1, 1, PAGE_SIZE, n_heads, V_DIM_A),
                            lambda b, st, pt, p=p: (layer, pt[b, st * pps + p], 0, 0, 0))

    in_specs = [page_spec(p) for p in range(pps)] + [page_spec(p) for p in range(pps)] + [
        pl.BlockSpec((SAMPLE_PAD, att_w), lambda b, st, pt: (meta_blocks + b, 0)),
        pl.BlockSpec((SAMPLE_PAD, att_w), lambda b, st, pt: (meta_blocks + b, 1)),
        pl.BlockSpec((SAMPLE_PAD, att_w), lambda b, st, pt: (meta_blocks + b, 2)),
        pl.BlockSpec((1, LANES), lambda b, st, pt: (0, 0)),
        pl.BlockSpec((1, LANES), lambda b, st, pt: (0, 0)),
        pl.BlockSpec((1, LANES), lambda b, st, pt: (0, 0)),
        pl.BlockSpec(lambda_qk.shape, lambda b, st, pt: (0, 0)),
        pl.BlockSpec((rows, LANES), lambda b, st, pt: (0, 0)),
    ]
    kernel = functools.partial(_decode_kernel, lam_init=lam_init, past_len=float(past_len),
                               n_heads=n_heads, dec_seq=dec_seq)
    grid_spec = pltpu.PrefetchScalarGridSpec(
        num_scalar_prefetch=1,
        grid=(dec_batch, n_pages // pps),
        in_specs=in_specs,
        out_specs=[
            pl.BlockSpec((SAMPLE_PAD, att_w), lambda b, st, pt: (b, 0)),
            pl.BlockSpec((SAMPLE_PAD, att_w), lambda b, st, pt: (b, 0)),
        ],
        scratch_shapes=[
            pltpu.VMEM((rows, V_DIM_A), F32),
            pltpu.VMEM((LANES, V_DIM_A), F32),
            pltpu.VMEM((LANES, V_DIM_A), F32),
            pltpu.VMEM((rows, pps * page_rows), F32),
            pltpu.VMEM((rows, LANES), F32),
            pltpu.VMEM((rows, LANES), F32),
            pltpu.VMEM((rows, V_DIM_A), F32),
        ],
    )
    return pl.pallas_call(
        kernel,
        grid_spec=grid_spec,
        out_shape=[
            jax.ShapeDtypeStruct((dec_batch * SAMPLE_PAD, att_w), F32),
            jax.ShapeDtypeStruct((dec_batch * SAMPLE_PAD, att_w), F32),
        ],
        compiler_params=_params("parallel", "arbitrary"),
        name="decode_attn",
    )(page_table, *([cache_k] * pps), *([cache_v] * pps), z_s, z_s, z_s, qg2, kg2, ag, lambda_qk, slope_rows)


def _log_sigmoid(x):
    return jnp.minimum(x, 0.0) - jnp.log(1.0 + jnp.exp(-jnp.abs(x)))


def _gate_columns(g, gb, hm):
    lane = lax.broadcasted_iota(jnp.int32, g.shape, 1)
    gg = g + gb
    ig = jnp.sum(jnp.where(lane == hm, gg, 0.0), axis=-1, keepdims=True)
    fg = jnp.sum(jnp.where(lane == hm + N_HEADS_M, gg, 0.0), axis=-1, keepdims=True)
    return ig, _log_sigmoid(fg)


def _conv_silu(ubuf_ref, start, rows, cw, cb):
    acc = cb
    for j in range(CONV_W):
        acc = acc + ubuf_ref[start + 5 + j:start + 5 + j + rows, :] * cw[j:j + 1, :]
    return acc * jax.nn.sigmoid(acc)


def _mlstm_chunk(q, k, v, lf_c, ig_c, c0, n0, m0, want_h):
    L = q.shape[0]
    row = lax.broadcasted_iota(jnp.int32, (L, L), 0)
    col = lax.broadcasted_iota(jnp.int32, (L, L), 1)
    eye = row == col
    tril = col <= row
    lf_r = jnp.sum(jnp.where(eye, lf_c, 0.0), axis=0, keepdims=True)
    ig_r = jnp.sum(jnp.where(eye, ig_c, 0.0), axis=0, keepdims=True)
    b_c = jnp.sum(jnp.where(tril, lf_r, 0.0), axis=1, keepdims=True)
    b_r = jnp.sum(jnp.where(row <= col, lf_c, 0.0), axis=0, keepdims=True)
    d = jnp.where(tril, b_c - b_r + ig_r, NEG_BIG)
    inter = b_c + m0
    m_c = jnp.maximum(inter, jnp.max(d, axis=1, keepdims=True))
    b_l = b_c[L - 1:L]
    m_end = m_c[L - 1:L]
    w_c = jnp.exp(b_l + m0 - m_end)
    w_s = jnp.exp(b_l - b_c + ig_c - m_end)
    kw = k.astype(F32) * w_s
    c1 = w_c * c0 + _dot_tn(kw.astype(BF16), v)
    n1 = w_c * n0 + jnp.sum(kw, axis=0, keepdims=True)
    if not want_h:
        return c1, n1, m_end, None
    s = _dot_nt(q, k) * jnp.exp(d - m_c)
    w_i = jnp.exp(inter - m_c)
    num = w_i * _dot(q, c0.astype(BF16)) + _dot(s.astype(BF16), v)
    den = w_i * jnp.sum(q.astype(F32) * n0, axis=1, keepdims=True) + jnp.sum(s, axis=1, keepdims=True)
    h = num / jnp.maximum(jnp.abs(den), jnp.exp(-m_c))
    return c1, n1, m_end, h


def _mlstm_prompt_kernel(u_ref, vm_ref, og_ref, gt_ref, um_ref, vmm_ref, gtm_ref,
                         cw_ref, cb_ref, wq_ref, wk_ref, gb_ref, go_ref,
                         mo_ref, c_ref, n_ref, m_ref, cv_ref,
                         ubuf, q_scr, k_scr, v_scr, ig_scr, lf_scr, c_scr, n_scr, m_scr, *, seq):
    L = MLSTM_CHUNK
    hm = pl.program_id(1)
    d = u_ref.shape[1]
    n_chunks = seq // L
    cw = cw_ref[...]
    cb = cb_ref[...]
    wq = wq_ref[0].astype(BF16)
    wk = wk_ref[0].astype(BF16)
    kscale = d ** -0.5

    ubuf[0:SUBLANES, :] = jnp.zeros((SUBLANES, d), F32)
    ubuf[SUBLANES:SUBLANES + N_META, :] = um_ref[...]
    ubuf[SUBLANES + N_META:, :] = u_ref[...]

    def put(dst0, rows, ua, vrows, g):
        ub = ua.astype(BF16)
        q_scr[dst0:dst0 + rows, :] = _dot(ub, wq).astype(BF16)
        k_scr[dst0:dst0 + rows, :] = (_dot(ub, wk) * kscale).astype(BF16)
        v_scr[dst0:dst0 + rows, :] = vrows.astype(BF16)
        ig, lf = _gate_columns(g, gb_ref[...], hm)
        ig_scr[dst0:dst0 + rows, :] = ig
        lf_scr[dst0:dst0 + rows, :] = lf

    q_scr[0:L, :] = jnp.zeros((L, d), BF16)
    k_scr[0:L, :] = jnp.zeros((L, d), BF16)
    v_scr[0:L, :] = jnp.zeros((L, d), BF16)
    ig_scr[0:L, :] = jnp.full((L, 1), NEG_BIG, F32)
    lf_scr[0:L, :] = jnp.zeros((L, 1), F32)
    put(0, N_META, _conv_silu(ubuf, 0, N_META, cw, cb), vmm_ref[...], gtm_ref[...])
    piece = 2 * L
    for r0 in range(0, seq, piece):
        put(L + r0, piece, _conv_silu(ubuf, N_META + r0, piece, cw, cb),
            vm_ref[r0:r0 + piece, :], gt_ref[r0:r0 + piece, :])

    c1, n1, m1, _ = _mlstm_chunk(q_scr[0:L, :], k_scr[0:L, :], v_scr[0:L, :], lf_scr[0:L, :], ig_scr[0:L, :],
                                 jnp.zeros((d, d), F32), jnp.zeros((1, d), F32), jnp.zeros((1, 1), F32), False)
    c_scr[...] = c1
    n_scr[...] = n1
    m_scr[...] = jnp.broadcast_to(m1, m_scr.shape)

    def body(c, carry):
        r0 = pl.multiple_of((c + 1) * L, L)
        rs = pl.ds(r0, L)
        c1, n1, m1, h = _mlstm_chunk(q_scr[rs, :], k_scr[rs, :], v_scr[rs, :], lf_scr[rs, :], ig_scr[rs, :],
                                     c_scr[...], n_scr[...], m_scr[:, :1], True)
        c_scr[...] = c1
        n_scr[...] = n1
        m_scr[...] = jnp.broadcast_to(m1, m_scr.shape)
        os = pl.ds(pl.multiple_of(c * L, L), L)
        mo_ref[os, :] = (_rms(h, go_ref[...]) * jax.nn.sigmoid(og_ref[os, :])).astype(BF16)
        return carry

    lax.fori_loop(0, n_chunks, body, 0)
    c_ref[0] = c_scr[...]
    n_ref[0] = n_scr[...]
    m_ref[0] = m_scr[...]
    total = SUBLANES + N_META + seq
    cv_ref[0] = ubuf[total - (CONV_W - 1):total, :]


def _mlstm_prompt(z, gates, z_s, gates_s, conv_w, conv_b, w_qm, w_km, gbias, g_out, batch, seq, col0):
    d = w_qm.shape[1]
    nh = w_qm.shape[0]
    cb0 = col0 // d
    width = nh * d
    kernel = functools.partial(_mlstm_prompt_kernel, seq=seq)
    t_scr = MLSTM_CHUNK + seq
    return pl.pallas_call(
        kernel,
        grid=(batch, nh),
        in_specs=[
            pl.BlockSpec((seq, d), lambda b, h: (b, cb0 + h)),
            pl.BlockSpec((seq, d), lambda b, h: (b, cb0 + nh + h)),
            pl.BlockSpec((seq, d), lambda b, h: (b, cb0 + 2 * nh + h)),
            pl.BlockSpec((seq, LANES), lambda b, h: (b, 0)),
            pl.BlockSpec((N_META, d), lambda b, h: (0, cb0 + h)),
            pl.BlockSpec((N_META, d), lambda b, h: (0, cb0 + nh + h)),
            pl.BlockSpec((N_META, LANES), lambda b, h: (0, 0)),
            pl.BlockSpec((CONV_W, d), lambda b, h: (0, h)),
            pl.BlockSpec((1, d), lambda b, h: (0, h)),
            pl.BlockSpec((1, d, d), lambda b, h: (h, 0, 0)),
            pl.BlockSpec((1, d, d), lambda b, h: (h, 0, 0)),
            pl.BlockSpec((1, LANES), lambda b, h: (0, 0)),
            pl.BlockSpec((1, d), lambda b, h: (0, 0)),
        ],
        out_specs=[
            pl.BlockSpec((seq, d), lambda b, h: (b, h)),
            pl.BlockSpec((1, d, d), lambda b, h: (b * nh + h, 0, 0)),
            pl.BlockSpec((1, 1, d), lambda b, h: (b * nh + h, 0, 0)),
            pl.BlockSpec((1, 1, LANES), lambda b, h: (b * nh + h, 0, 0)),
            pl.BlockSpec((1, CONV_W - 1, d), lambda b, h: (b, 0, h)),
        ],
        out_shape=[
            jax.ShapeDtypeStruct((batch * seq, width), BF16),
            jax.ShapeDtypeStruct((batch * nh, d, d), F32),
            jax.ShapeDtypeStruct((batch * nh, 1, d), F32),
            jax.ShapeDtypeStruct((batch * nh, 1, LANES), F32),
            jax.ShapeDtypeStruct((batch, CONV_W - 1, width), F32),
        ],
        scratch_shapes=[
            pltpu.VMEM((SUBLANES + N_META + seq, d), F32),
            pltpu.VMEM((t_scr, d), BF16),
            pltpu.VMEM((t_scr, d), BF16),
            pltpu.VMEM((t_scr, d), BF16),
            pltpu.VMEM((t_scr, 1), F32),
            pltpu.VMEM((t_scr, 1), F32),
            pltpu.VMEM((d, d), F32),
            pltpu.VMEM((1, d), F32),
            pltpu.VMEM((1, LANES), F32),
        ],
        compiler_params=_params("parallel", "parallel"),
        name="mlstm_prompt",
    )(z, z, z, gates, z_s, z_s, gates_s, conv_w, conv_b, w_qm, w_km, gbias, g_out)


def _mlstm_sample_kernel(u_ref, vm_ref, og_ref, gt_ref, sc_ref, c0_ref, n0_ref, m0_ref,
                         cw_ref, cb_ref, wq_ref, wk_ref, gb_ref, go_ref,
                         mo_ref, c_ref, n_ref, m_ref, cv_ref,
                         ubuf, q_scr, k_scr, v_scr, ig_scr, lf_scr, *, dec_seq):
    L = MLSTM_CHUNK
    P = SAMPLE_PAD
    hm = pl.program_id(1)
    d = u_ref.shape[1]
    wq = wq_ref[0].astype(BF16)
    wk = wk_ref[0].astype(BF16)

    ubuf[0:SUBLANES, :] = jnp.zeros((SUBLANES, d), F32)
    ubuf[SUBLANES - (CONV_W - 1):SUBLANES, :] = sc_ref[0, 0]
    ubuf[SUBLANES:SUBLANES + P, :] = u_ref[...]
    ubuf[SUBLANES + P:, :] = jnp.zeros((SUBLANES, d), F32)

    ua = _conv_silu(ubuf, 0, P, cw_ref[...], cb_ref[...]).astype(BF16)
    q_scr[...] = jnp.zeros((L, d), BF16)
    k_scr[...] = jnp.zeros((L, d), BF16)
    v_scr[...] = jnp.zeros((L, d), BF16)
    q_scr[0:2 * P, :] = jnp.concatenate([_dot(ua, wq), jnp.zeros((P, d), F32)], axis=0).astype(BF16)
    k_scr[0:2 * P, :] = jnp.concatenate([_dot(ua, wk) * d ** -0.5, jnp.zeros((P, d), F32)], axis=0).astype(BF16)
    v_scr[0:2 * P, :] = jnp.concatenate([vm_ref[...], jnp.zeros((P, d), F32)], axis=0).astype(BF16)
    ig, lf = _gate_columns(gt_ref[...], gb_ref[...], hm)
    tok = lax.broadcasted_iota(jnp.int32, (P, 1), 0)
    ig_scr[...] = jnp.full((L, 1), NEG_BIG, F32)
    lf_scr[...] = jnp.zeros((L, 1), F32)
    ig_scr[0:P, :] = jnp.where(tok < dec_seq, ig, NEG_BIG)
    lf_scr[0:P, :] = jnp.where(tok < dec_seq, lf, 0.0)

    c1, n1, m1, h = _mlstm_chunk(q_scr[...], k_scr[...], v_scr[...], lf_scr[...], ig_scr[...],
                                 c0_ref[0], n0_ref[0], m0_ref[0][:, :1], True)
    c_ref[0] = c1
    n_ref[0] = n1
    m_ref[0] = jnp.broadcast_to(m1, (1, LANES))
    mo_ref[...] = _rms(h[0:P], go_ref[...]) * jax.nn.sigmoid(og_ref[...])
    last = SUBLANES + dec_seq
    cv_ref[0, 0] = ubuf[last - (CONV_W - 1):last, :]


def _mlstm_sample(z_s, gates_s, state_conv, c0, n0, m0, conv_w, conv_b, w_qm, w_km, gbias, g_out,
                  dec_batch, dec_seq, col0):
    d = w_qm.shape[1]
    nh = w_qm.shape[0]
    cb0 = col0 // d
    width = nh * d
    mb = N_META // SAMPLE_PAD
    L = MLSTM_CHUNK
    kernel = functools.partial(_mlstm_sample_kernel, dec_seq=dec_seq)
    return pl.pallas_call(
        kernel,
        grid=(dec_batch, nh),
        in_specs=[
            pl.BlockSpec((SAMPLE_PAD, d), lambda b, h: (mb + b, cb0 + h)),
            pl.BlockSpec((SAMPLE_PAD, d), lambda b, h: (mb + b, cb0 + nh + h)),
            pl.BlockSpec((SAMPLE_PAD, d), lambda b, h: (mb + b, cb0 + 2 * nh + h)),
            pl.BlockSpec((SAMPLE_PAD, LANES), lambda b, h: (mb + b, 0)),
            pl.BlockSpec((1, 1, CONV_W - 1, d), lambda b, h: (0, b, 0, h)),
            pl.BlockSpec((1, d, d), lambda b, h: (b * nh + h, 0, 0)),
            pl.BlockSpec((1, 1, d), lambda b, h: (b * nh + h, 0, 0)),
            pl.BlockSpec((1, 1, LANES), lambda b, h: (b * nh + h, 0, 0)),
            pl.BlockSpec((CONV_W, d), lambda b, h: (0, h)),
            pl.BlockSpec((1, d), lambda b, h: (0, h)),
            pl.BlockSpec((1, d, d), lambda b, h: (h, 0, 0)),
            pl.BlockSpec((1, d, d), lambda b, h: (h, 0, 0)),
            pl.BlockSpec((1, LANES), lambda b, h: (0, 0)),
            pl.BlockSpec((1, d), lambda b, h: (0, 0)),
        ],
        out_specs=[
            pl.BlockSpec((SAMPLE_PAD, d), lambda b, h: (b, h)),
            pl.BlockSpec((1, d, d), lambda b, h: (b * nh + h, 0, 0)),
            pl.BlockSpec((1, 1, d), lambda b, h: (b * nh + h, 0, 0)),
            pl.BlockSpec((1, 1, LANES), lambda b, h: (b * nh + h, 0, 0)),
            pl.BlockSpec((1, 1, CONV_W - 1, d), lambda b, h: (0, b, 0, h)),
        ],
        out_shape=[
            jax.ShapeDtypeStruct((dec_batch * SAMPLE_PAD, width), F32),
            jax.ShapeDtypeStruct((dec_batch * nh, d, d), F32),
            jax.ShapeDtypeStruct((dec_batch * nh, 1, d), F32),
            jax.ShapeDtypeStruct((dec_batch * nh, 1, LANES), F32),
            jax.ShapeDtypeStruct((1, dec_batch, CONV_W - 1, width), F32),
        ],
        scratch_shapes=[
            pltpu.VMEM((3 * SUBLANES, d), F32),
            pltpu.VMEM((L, d), BF16),
            pltpu.VMEM((L, d), BF16),
            pltpu.VMEM((L, d), BF16),
            pltpu.VMEM((L, 1), F32),
            pltpu.VMEM((L, 1), F32),
        ],
        compiler_params=_params("parallel", "parallel"),
        name="mlstm_sample",
    )(z_s, z_s, z_s, gates_s, state_conv, c0, n0, m0, conv_w, conv_b, w_qm, w_km, gbias, g_out)


def _layer(l, x_p, x_s, cache_k, cache_v, state_C, state_n, state_m, state_conv, page_table,
           norm_mix, w_in, q_norm, k_norm, lambda_qk, attn_out_norm, conv_w, conv_b, w_qm, w_km,
           gate_bias, mlstm_out_norm, w_out, norm_ffn, w_up, w_down, *, batch, seq, dec_batch, dec_seq):
    d_model = x_p.shape[1]
    nh_m = w_qm.shape[0]
    d_m = w_qm.shape[1]
    mlstm_w = nh_m * d_m
    att_w = d_model - mlstm_w
    n_heads_a = att_w // V_DIM_A
    main_w = 3 * att_w + 3 * mlstm_w
    lam_init = 0.8 - 0.6 * math.exp(-0.3 * l)
    slopes = 2.0 ** (-8.0 * jnp.arange(1, n_heads_a + 1, dtype=F32) / n_heads_a)

    w_main = w_in[:, :main_w].astype(BF16)
    wg = jnp.pad(w_in[:, main_w:], ((0, 0), (0, LANES - 2 * nh_m)))
    wgh = wg.astype(BF16)
    wgl = (wg - wgh.astype(F32)).astype(BF16)
    g_mix = norm_mix.reshape(1, d_model)
    qg2 = jnp.tile(q_norm, 2).reshape(1, V_DIM_A)
    kg2 = jnp.tile(k_norm, 2).reshape(1, V_DIM_A)
    ag = attn_out_norm.reshape(1, V_DIM_A)
    gbias = jnp.pad(gate_bias.reshape(1, 2 * nh_m), ((0, 0), (0, LANES - 2 * nh_m)))
    cb = conv_b.reshape(1, mlstm_w)
    g_out = mlstm_out_norm.reshape(1, d_m)
    wo_a = w_out[:att_w].astype(BF16)
    wo_m = w_out[att_w:].astype(BF16)
    g_ffn = norm_ffn.reshape(1, d_model)
    wu = w_up.astype(BF16)
    wd = w_down.astype(BF16)

    rows_s = x_s.shape[0]
    z, gates = _inproj(x_p, g_mix, w_main, wgh, wgl, tm=1024, tn=512)
    z_s, gates_s = _inproj(x_s, g_mix, w_main, wgh, wgl, tm=rows_s, tn=512)

    k_p, v_p, a_p = _prompt_attention(z, z_s, qg2, kg2, ag, lambda_qk, slopes, lam_init, batch, seq)
    a_s, ks = _decode_attention(page_table, cache_k, cache_v, l, z_s, qg2, kg2, ag, lambda_qk, slopes,
                                lam_init, dec_seq)

    m_p, c_p, n_p, mm_p, cv_p = _mlstm_prompt(z, gates, z_s, gates_s, conv_w, cb, w_qm, w_km, gbias, g_out,
                                              batch, seq, 3 * att_w)
    c0 = state_C.reshape(dec_batch * nh_m, d_m, d_m)
    n0 = state_n.reshape(dec_batch * nh_m, 1, d_m)
    m0 = jnp.broadcast_to(state_m.reshape(dec_batch * nh_m, 1, 1), (dec_batch * nh_m, 1, LANES))
    m_s, c_s, n_s, mm_s, cv_s = _mlstm_sample(z_s, gates_s, state_conv, c0, n0, m0, conv_w, cb, w_qm, w_km,
                                              gbias, g_out, dec_batch, dec_seq, 3 * att_w)

    x_sample_rows = x_s[N_META:]
    x1_p = _outproj(x_p, a_p, m_p, wo_a, wo_m, tm=1024, tn=512)
    x1_s = _outproj(x_sample_rows, a_s, m_s, wo_a, wo_m, tm=x_sample_rows.shape[0], tn=512)
    y_p = _ffn(x1_p, g_ffn, wu, wd, tm=1024, tf=512)
    y_s = _ffn(x1_s, g_ffn, wu, wd, tm=x1_s.shape[0], tf=512)

    def unpad(a):
        return a.reshape(dec_batch, SAMPLE_PAD, a.shape[-1])[:, :dec_seq]

    outs = dict(
        y_p=y_p, y_s=unpad(y_s),
        k_p=k_p.reshape(batch, seq + N_META, n_heads_a, V_DIM_A),
        v_p=v_p.reshape(batch, seq + N_META, n_heads_a, V_DIM_A),
        c_p=c_p.reshape(batch, nh_m, d_m, d_m), n_p=n_p.reshape(batch, nh_m, d_m),
        m_p=mm_p[:, 0, 0].reshape(batch, nh_m), cv_p=cv_p,
        k_s=unpad(ks).reshape(dec_batch, dec_seq, n_heads_a, V_DIM_A),
        v_s=unpad(z_s[N_META:, 2 * att_w:3 * att_w]).reshape(dec_batch, dec_seq, n_heads_a, V_DIM_A),
        c_s=c_s.reshape(dec_batch, nh_m, d_m, d_m), n_s=n_s.reshape(dec_batch, nh_m, d_m),
        m_s=mm_s[:, 0, 0].reshape(dec_batch, nh_m), cv_s=cv_s[0],
    )
    return outs


def kernel(x_prompt, x_sample, cache_k, cache_v, state_C, state_n, state_m, state_conv, page_table, meta_tokens, norm_mix, w_in, q_norm, k_norm, lambda_qk, attn_out_norm, conv_w, conv_b, w_qm, w_km, gate_bias, mlstm_out_norm, w_out, norm_ffn, w_up, w_down):
    batch, seq, d_model = x_prompt.shape
    dec_batch, dec_seq, _ = x_sample.shape
    depth = w_in.shape[0]
    assert depth == 1, "prompt rows are kept without the meta tokens, which is only valid for one layer"
    assert dec_seq <= SAMPLE_PAD and seq % ATT_QBLOCK == 0 and seq % (2 * MLSTM_CHUNK) == 0
    assert ATT_QBLOCK % ATT_KBLOCK == 0

    x_p = x_prompt.reshape(batch * seq, d_model)
    xs_pad = jnp.pad(x_sample, ((0, 0), (0, SAMPLE_PAD - dec_seq), (0, 0))).reshape(dec_batch * SAMPLE_PAD, d_model)
    x_s = jnp.concatenate([meta_tokens.astype(x_prompt.dtype), xs_pad], axis=0)

    l = 0
    o = _layer(l, x_p, x_s, cache_k, cache_v, state_C[l], state_n[l], state_m[l], state_conv[l:l + 1],
               page_table, norm_mix[l], w_in[l], q_norm[l], k_norm[l], lambda_qk[l], attn_out_norm[l],
               conv_w[l], conv_b[l], w_qm[l], w_km[l], gate_bias[l], mlstm_out_norm[l], w_out[l],
               norm_ffn[l], w_up[l], w_down[l], batch=batch, seq=seq, dec_batch=dec_batch, dec_seq=dec_seq)
    st = lambda a: a[None]
    return (o["y_p"].reshape(batch, seq, d_model), o["y_s"],
            st(o["k_p"]), st(o["v_p"]), st(o["c_p"]), st(o["n_p"]), st(o["m_p"]), st(o["cv_p"]),
            st(o["k_s"]), st(o["v_s"]), st(o["c_s"]), st(o["n_s"]), st(o["m_s"]), st(o["cv_s"]))
```

```python
import functools
import math

import jax
import jax.numpy as jnp
from jax import lax
from jax.experimental import pallas as pl
from jax.experimental.pallas import tpu as pltpu

F32 = jnp.float32
BF16 = jnp.bfloat16

N_META = 16
HEAD_DIM_A = 64
V_DIM_A = 2 * HEAD_DIM_A
N_HEADS_M = 4
CONV_W = 4
PAGE_SIZE = 128
EPS = 1e-6

LANES = 128
SUBLANES = 8
VMEM_LIMIT_BYTES = 56 * 1024 * 1024

LOG2E = 1.4426950408889634
NEG_BIG = -1e30

SAMPLE_PAD = SUBLANES
MLSTM_CHUNK = 256
ATT_QBLOCK = 512
ATT_KBLOCK = 256
ATT_HEADS_PER_STEP = 2
DEC_PAGES_PER_STEP = 8


def _dot(a, b):
    return jnp.dot(a, b, preferred_element_type=F32)


def _dot_nt(a, b):
    return lax.dot_general(a, b, (((1,), (1,)), ((), ())), preferred_element_type=F32)


def _dot_tn(a, b):
    return lax.dot_general(a, b, (((0,), (0,)), ((), ())), preferred_element_type=F32)


def _rms(x, g):
    return x * lax.rsqrt(jnp.mean(x * x, axis=-1, keepdims=True) + EPS) * g


def _params(*sem):
    return pltpu.CompilerParams(dimension_semantics=sem, vmem_limit_bytes=VMEM_LIMIT_BYTES)


def _inproj_kernel(x_ref, g_ref, w_ref, wgh_ref, wgl_ref, z_ref, gate_ref, xn_scr):
    @pl.when(pl.program_id(1) == 0)
    def _():
        xn = _rms(x_ref[...], g_ref[...])
        xh = xn.astype(BF16)
        xl = (xn - xh.astype(F32)).astype(BF16)
        xn_scr[...] = xh
        wgh = wgh_ref[...]
        gate_ref[...] = _dot(xh, wgh) + (_dot(xl, wgh) + _dot(xh, wgl_ref[...]))

    z_ref[...] = _dot(xn_scr[...], w_ref[...].astype(BF16))


def _inproj(x, g, w, wgh, wgl, tm, tn):
    rows, d = x.shape
    n = (w.shape[1] // tn) * tn
    return pl.pallas_call(
        _inproj_kernel,
        grid=(rows // tm, n // tn),
        in_specs=[
            pl.BlockSpec((tm, d), lambda i, j: (i, 0)),
            pl.BlockSpec((1, d), lambda i, j: (0, 0)),
            pl.BlockSpec((d, tn), lambda i, j: (0, j)),
            pl.BlockSpec((d, LANES), lambda i, j: (0, 0)),
            pl.BlockSpec((d, LANES), lambda i, j: (0, 0)),
        ],
        out_specs=[
            pl.BlockSpec((tm, tn), lambda i, j: (i, j)),
            pl.BlockSpec((tm, LANES), lambda i, j: (i, 0)),
        ],
        out_shape=[jax.ShapeDtypeStruct((rows, n), F32), jax.ShapeDtypeStruct((rows, LANES), F32)],
        scratch_shapes=[pltpu.VMEM((tm, d), BF16)],
        compiler_params=_params("parallel", "arbitrary"),
        name="inproj",
    )(x, g, w, wgh, wgl)


def _outproj_kernel(x_ref, a_ref, m_ref, wa_ref, wm_ref, o_ref):
    a = a_ref[...].astype(BF16)
    m = m_ref[...].astype(BF16)
    o_ref[...] = x_ref[...] + (_dot(a, wa_ref[...].astype(BF16)) + _dot(m, wm_ref[...].astype(BF16)))


def _outproj(x, a, m, wa, wm, tm, tn):
    rows, d = x.shape
    ka = a.shape[1]
    km = m.shape[1]
    return pl.pallas_call(
        _outproj_kernel,
        grid=(rows // tm, d // tn),
        in_specs=[
            pl.BlockSpec((tm, tn), lambda i, j: (i, j)),
            pl.BlockSpec((tm, ka), lambda i, j: (i, 0)),
            pl.BlockSpec((tm, km), lambda i, j: (i, 0)),
            pl.BlockSpec((ka, tn), lambda i, j: (0, j)),
            pl.BlockSpec((km, tn), lambda i, j: (0, j)),
        ],
        out_specs=pl.BlockSpec((tm, tn), lambda i, j: (i, j)),
        out_shape=jax.ShapeDtypeStruct((rows, d), F32),
        compiler_params=_params("parallel", "arbitrary"),
        name="outproj",
    )(x, a, m, wa, wm)


def _ffn_kernel(x_ref, g_ref, wu_ref, wd_ref, o_ref, xn_scr):
    @pl.when(pl.program_id(1) == 0)
    def _():
        x = x_ref[...]
        xn_scr[...] = _rms(x, g_ref[...]).astype(BF16)
        o_ref[...] = x

    h = _dot(xn_scr[...], wu_ref[...].astype(BF16))
    h = jnp.square(jnp.maximum(h, 0.0))
    o_ref[...] += _dot(h.astype(BF16), wd_ref[...].astype(BF16))


def _ffn(x, g, wu, wd, tm, tf):
    rows, d = x.shape
    dff = wu.shape[1]
    return pl.pallas_call(
        _ffn_kernel,
        grid=(rows // tm, dff // tf),
        in_specs=[
            pl.BlockSpec((tm, d), lambda i, f: (i, 0)),
            pl.BlockSpec((1, d), lambda i, f: (0, 0)),
            pl.BlockSpec((d, tf), lambda i, f: (0, f)),
            pl.BlockSpec((tf, d), lambda i, f: (f, 0)),
        ],
        out_specs=pl.BlockSpec((tm, d), lambda i, f: (i, 0)),
        out_shape=jax.ShapeDtypeStruct((rows, d), F32),
        scratch_shapes=[pltpu.VMEM((tm, d), BF16)],
        compiler_params=_params("parallel", "arbitrary"),
        name="ffn",
    )(x, g, wu, wd)


def _qk_norm(x, g):
    lane = lax.broadcasted_iota(jnp.int32, x.shape, 1)
    lo = lane < HEAD_DIM_A
    x2 = x * x
    s_lo = jnp.sum(jnp.where(lo, x2, 0.0), axis=-1, keepdims=True)
    s_hi = jnp.sum(jnp.where(lo, 0.0, x2), axis=-1, keepdims=True)
    ms = jnp.where(lo, s_lo, s_hi) * (1.0 / HEAD_DIM_A)
    return x * lax.rsqrt(ms + EPS) * g


def _diff_lambda(l, lam_init):
    s01 = jnp.sum(l[0:1] * l[1:2], axis=-1, keepdims=True)
    s23 = jnp.sum(l[2:3] * l[3:4], axis=-1, keepdims=True)
    return jnp.exp(s01) - jnp.exp(s23) + lam_init


def _split3(x):
    b1 = x.astype(BF16).astype(F32)
    r1 = x - b1
    b2 = r1.astype(BF16).astype(F32)
    return b1, b2, r1 - b2


def _augment_k(kn, bias, half):
    lane = lax.broadcasted_iota(jnp.int32, kn.shape, 1)
    b1, b2, b3 = _split3(bias)
    base = HEAD_DIM_A if half == 0 else 0
    aug = jnp.where(lane == base, b1, jnp.where(lane == base + 1, b2, jnp.where(lane == base + 2, b3, 0.0)))
    keep = (lane < HEAD_DIM_A) if half == 0 else (lane >= HEAD_DIM_A)
    return jnp.where(keep, kn, aug).astype(BF16)


def _augment_q(qn, half):
    lane = lax.broadcasted_iota(jnp.int32, qn.shape, 1)
    base = HEAD_DIM_A if half == 0 else 0
    ones = jnp.where((lane >= base) & (lane < base + 3), 1.0, 0.0)
    keep = (lane < HEAD_DIM_A) if half == 0 else (lane >= HEAD_DIM_A)
    return jnp.where(keep, qn, ones)


def _attn_kernel(q_ref, k_ref, v_ref, km_ref, vm_ref, qg_ref, kg_ref, ag_ref, lam_ref, sl_ref,
                 ko_ref, vo_ref, a_ref,
                 qt_scr, ka_scr, vt_scr, kma_scr, vtm_scr, m_scr, l_scr, ot_scr, *, lam_init, seq):
    hps = ATT_HEADS_PER_STEP
    tq = ATT_QBLOCK
    tk = ATT_KBLOCK
    qg = qg_ref[...]
    kg = kg_ref[...]
    lam = _diff_lambda(lam_ref[...], lam_init)

    row_s = lax.broadcasted_iota(jnp.int32, (seq, LANES), 0)
    row_m = lax.broadcasted_iota(jnp.int32, (N_META, LANES), 0)
    kma_scr[...] = jnp.zeros_like(kma_scr)
    slopes2 = []
    for hh in range(hps):
        ln = slice(hh * LANES, (hh + 1) * LANES)
        slope2 = sl_ref[0, hh:hh + 1, :] * LOG2E
        slopes2.append(slope2[:, :1])
        kn = _qk_norm(k_ref[:, ln], kg)
        kmn = _qk_norm(km_ref[:, ln], kg)
        ko_ref[0, 0:N_META, ln] = kmn
        ko_ref[0, N_META:, ln] = kn
        v = v_ref[:, ln]
        vmeta = vm_ref[:, ln]
        vo_ref[0, 0:N_META, ln] = vmeta
        vo_ref[0, N_META:, ln] = v
        for kb in range(seq // tk):
            vt_scr[hh, kb] = v[kb * tk:(kb + 1) * tk, :].T.astype(BF16)
        vpad = jnp.concatenate([vmeta, jnp.zeros((LANES - N_META, LANES), F32)], axis=0)
        vtm_scr[hh] = vpad.T.astype(BF16)
        qn = _qk_norm(q_ref[:, ln], qg) * (HEAD_DIM_A ** -0.5 * LOG2E)
        bias = (row_s & (tk - 1)).astype(F32) * slope2
        bias_m = row_m.astype(F32) * slope2
        for mm in range(2):
            ka_scr[hh, mm] = _augment_k(kn, bias, mm)
            kma_scr[hh, mm, 0:N_META, :] = _augment_k(kmn, bias_m, mm)
            qa = _augment_q(qn, mm)
            for qb in range(seq // tq):
                qt_scr[hh, mm, qb] = qa[qb * tq:(qb + 1) * tq, :].T.astype(BF16)

    rel = (lax.broadcasted_iota(jnp.int32, (tk, tq), 0) - lax.broadcasted_iota(jnp.int32, (tk, tq), 1))
    meta_valid = lax.broadcasted_iota(jnp.int32, (LANES, tq), 0) < N_META
    chains = [(hh, mm) for hh in range(hps) for mm in range(2)]
    kpq = tq // tk

    def q_block(qi, carry):
        q0 = pl.multiple_of(qi * tq, tq)

        sds = [[_dot(ka_scr[hh, mm, pl.ds(q0 + d * tk, tk), :], qt_scr[hh, mm, qi]) for d in range(kpq)]
               for hh, mm in chains]
        sms = [_dot(kma_scr[hh, mm], qt_scr[hh, mm, qi]) for hh, mm in chains]
        pds, pms = [], []
        for (hh, mm), sd, sm in zip(chains, sds, sms):
            ch = hh * 2 + mm
            sd = [jnp.where(rel <= -d * tk, s + slopes2[hh] * float(d * tk), NEG_BIG) for d, s in enumerate(sd)]
            cm = slopes2[hh] * (-N_META - q0).astype(F32)
            sm = jnp.where(meta_valid, sm + cm, NEG_BIG)
            m = jnp.max(sm, axis=0, keepdims=True)
            for s in sd:
                m = jnp.maximum(m, jnp.max(s, axis=0, keepdims=True))
            pm = jnp.exp2(sm - m)
            l = jnp.sum(pm, axis=0, keepdims=True)
            pd = [jnp.exp2(s - m) for s in sd]
            for p in pd:
                l = l + jnp.sum(p, axis=0, keepdims=True)
            m_scr[ch] = jnp.broadcast_to(m, (SUBLANES, tq))
            l_scr[ch] = jnp.broadcast_to(l, (SUBLANES, tq))
            pds.append([p.astype(BF16) for p in pd])
            pms.append(pm.astype(BF16))
        for (hh, mm), pd, pm in zip(chains, pds, pms):
            o = _dot(vtm_scr[hh], pm)
            for d, p in enumerate(pd):
                o = o + _dot(vt_scr[hh, qi * kpq + d], p)
            ot_scr[hh * 2 + mm] = o

        def k_block(j, carry):
            k0 = pl.multiple_of(j * tk, tk)
            ss = [_dot(ka_scr[hh, mm, pl.ds(k0, tk), :], qt_scr[hh, mm, qi]) for hh, mm in chains]
            ps, alphas = [], []
            for (hh, mm), s in zip(chains, ss):
                ch = hh * 2 + mm
                c = slopes2[hh] * (k0 - q0).astype(F32)
                m_old = m_scr[ch][0:1]
                m_new = jnp.maximum(m_old, jnp.max(s, axis=0, keepdims=True) + c)
                p = jnp.exp2(s - (m_new - c))
                alpha = jnp.exp2(m_old - m_new)
                l = alpha * l_scr[ch][0:1] + jnp.sum(p, axis=0, keepdims=True)
                m_scr[ch] = jnp.broadcast_to(m_new, (SUBLANES, tq))
                l_scr[ch] = jnp.broadcast_to(l, (SUBLANES, tq))
                ps.append(p.astype(BF16))
                alphas.append(alpha)
            for (hh, mm), p, alpha in zip(chains, ps, alphas):
                ch = hh * 2 + mm
                ot_scr[ch] = alpha * ot_scr[ch] + _dot(vt_scr[hh, j], p)
            return carry

        lax.fori_loop(0, qi * kpq, k_block, 0)

        for hh in range(hps):
            ln = slice(hh * LANES, (hh + 1) * LANES)
            o1 = ot_scr[2 * hh] / l_scr[2 * hh][0:1]
            o2 = ot_scr[2 * hh + 1] / l_scr[2 * hh + 1][0:1]
            a = (o1 - lam * o2).T
            a = _rms(a, ag_ref[...]) * (1.0 - lam_init)
            a_ref[pl.ds(q0, tq), ln] = a.astype(BF16)
        return carry

    lax.fori_loop(0, seq // tq, q_block, 0)


def _prompt_attention(z, z_s, qg2, kg2, ag, lambda_qk, slopes, lam_init, batch, seq):
    hps = ATT_HEADS_PER_STEP
    n_heads = slopes.shape[0]
    width = hps * LANES
    groups = n_heads // hps
    att_w = n_heads * V_DIM_A
    t_all = seq + N_META
    sl = jnp.broadcast_to(slopes.reshape(groups, hps, 1), (groups, hps, LANES))
    kernel = functools.partial(_attn_kernel, lam_init=lam_init, seq=seq)
    return pl.pallas_call(
        kernel,
        grid=(batch, groups),
        in_specs=[
            pl.BlockSpec((seq, width), lambda b, g: (b, g)),
            pl.BlockSpec((seq, width), lambda b, g: (b, groups + g)),
            pl.BlockSpec((seq, width), lambda b, g: (b, 2 * groups + g)),
            pl.BlockSpec((N_META, width), lambda b, g: (0, groups + g)),
            pl.BlockSpec((N_META, width), lambda b, g: (0, 2 * groups + g)),
            pl.BlockSpec((1, LANES), lambda b, g: (0, 0)),
            pl.BlockSpec((1, LANES), lambda b, g: (0, 0)),
            pl.BlockSpec((1, LANES), lambda b, g: (0, 0)),
            pl.BlockSpec(lambda_qk.shape, lambda b, g: (0, 0)),
            pl.BlockSpec((1, hps, LANES), lambda b, g: (g, 0, 0)),
        ],
        out_specs=[
            pl.BlockSpec((1, t_all, width), lambda b, g: (b, 0, g)),
            pl.BlockSpec((1, t_all, width), lambda b, g: (b, 0, g)),
            pl.BlockSpec((seq, width), lambda b, g: (b, g)),
        ],
        out_shape=[
            jax.ShapeDtypeStruct((batch, t_all, att_w), F32),
            jax.ShapeDtypeStruct((batch, t_all, att_w), F32),
            jax.ShapeDtypeStruct((batch * seq, att_w), BF16),
        ],
        scratch_shapes=[
            pltpu.VMEM((hps, 2, seq // ATT_QBLOCK, LANES, ATT_QBLOCK), BF16),
            pltpu.VMEM((hps, 2, seq, LANES), BF16),
            pltpu.VMEM((hps, seq // ATT_KBLOCK, LANES, ATT_KBLOCK), BF16),
            pltpu.VMEM((hps, 2, LANES, LANES), BF16),
            pltpu.VMEM((hps, LANES, LANES), BF16),
            pltpu.VMEM((2 * hps, SUBLANES, ATT_QBLOCK), F32),
            pltpu.VMEM((2 * hps, SUBLANES, ATT_QBLOCK), F32),
            pltpu.VMEM((2 * hps, LANES, ATT_QBLOCK), F32),
        ],
        compiler_params=_params("parallel", "parallel"),
        name="prompt_attn",
    )(z, z, z, z_s, z_s, qg2, kg2, ag, lambda_qk, sl)


def _decode_kernel(pt_ref, *refs, lam_init, past_len, n_heads, dec_seq):
    del pt_ref
    pps = DEC_PAGES_PER_STEP
    k_refs = refs[:pps]
    v_refs = refs[pps:2 * pps]
    (zq_ref, zk_ref, zv_ref, qg_ref, kg_ref, ag_ref, lam_ref, slc_ref,
     a_ref, ks_ref, qall_scr, knf_scr, vnf_scr, mb_scr, m_scr, l_scr, acc_scr) = refs[2 * pps:]
    st = pl.program_id(1)
    n_st = pl.num_programs(1)
    rows = 2 * n_heads * SAMPLE_PAD
    page_rows = PAGE_SIZE * n_heads
    width = pps * page_rows
    hshift = n_heads.bit_length() - 1
    slope_c = slc_ref[:, :1] * LOG2E
    row_i = lax.broadcasted_iota(jnp.int32, (rows, 1), 0)
    t_c = (row_i & (SAMPLE_PAD - 1)).astype(F32)
    h_c = row_i >> (hshift + 1)

    @pl.when(st == 0)
    def _():
        qall_scr[...] = jnp.zeros_like(qall_scr)
        knf_scr[...] = jnp.zeros_like(knf_scr)
        vnf_scr[...] = jnp.zeros_like(vnf_scr)
        for h in range(n_heads):
            ln = slice(h * LANES, (h + 1) * LANES)
            qn = _qk_norm(zq_ref[:, ln], qg_ref[...]) * (HEAD_DIM_A ** -0.5 * LOG2E)
            lane = lax.broadcasted_iota(jnp.int32, qn.shape, 1)
            qall_scr[(2 * h) * SAMPLE_PAD:(2 * h + 1) * SAMPLE_PAD, :] = jnp.where(lane < HEAD_DIM_A, qn, 0.0)
            qall_scr[(2 * h + 1) * SAMPLE_PAD:(2 * h + 2) * SAMPLE_PAD, :] = jnp.where(lane < HEAD_DIM_A, 0.0, qn)
            kn = _qk_norm(zk_ref[:, ln], kg_ref[...])
            ks_ref[:, ln] = kn
            knf_scr[h * SAMPLE_PAD:(h + 1) * SAMPLE_PAD, :] = kn
            vnf_scr[h * SAMPLE_PAD:(h + 1) * SAMPLE_PAD, :] = zv_ref[:, ln]
        lane = lax.broadcasted_iota(jnp.int32, (rows, width), 1)
        tok = (lane >> hshift).astype(F32)
        mb_scr[...] = jnp.where((lane & (n_heads - 1)) == h_c, slope_c * tok, NEG_BIG)
        m_scr[...] = jnp.full_like(m_scr, NEG_BIG)
        l_scr[...] = jnp.zeros_like(l_scr)
        acc_scr[...] = jnp.zeros_like(acc_scr)

    qall = qall_scr[...].astype(BF16)

    def update(s, c, pv):
        smax = jnp.max(s, axis=1, keepdims=True) + c
        m_old = m_scr[:, :1]
        m_new = jnp.maximum(m_old, smax)
        p = jnp.exp2(s - (m_new - c))
        alpha = jnp.exp2(m_old - m_new)
        l_scr[...] = jnp.broadcast_to(alpha * l_scr[:, :1] + jnp.sum(p, axis=1, keepdims=True), l_scr.shape)
        acc_scr[...] = alpha * acc_scr[...] + pv(p.astype(BF16))
        m_scr[...] = jnp.broadcast_to(m_new, m_scr.shape)

    def flat(ref):
        return ref[0, 0].reshape(page_rows, V_DIM_A).astype(BF16)

    k0 = (st * (pps * PAGE_SIZE)).astype(F32)
    s = jnp.concatenate([_dot_nt(qall, flat(k_refs[p])) for p in range(pps)], axis=1) + mb_scr[...]
    c = slope_c * (k0 - (past_len + t_c))

    def pv_pages(p):
        out = _dot(p[:, 0:page_rows], flat(v_refs[0]))
        for i in range(1, pps):
            out = out + _dot(p[:, i * page_rows:(i + 1) * page_rows], flat(v_refs[i]))
        return out

    update(s, c, pv_pages)

    @pl.when(st == n_st - 1)
    def _():
        sn = _dot_nt(qall, knf_scr[...].astype(BF16))
        lane = lax.broadcasted_iota(jnp.int32, (rows, LANES), 1)
        tk = lane & (SAMPLE_PAD - 1)
        tkf = tk.astype(F32)
        valid = ((lane >> 3) == h_c) & (tkf <= t_c) & (tk < dec_seq)
        sn = jnp.where(valid, sn + slope_c * tkf, NEG_BIG)
        update(sn, slope_c * (0.0 - t_c), lambda p: _dot(p, vnf_scr[...].astype(BF16)))

        lam = _diff_lambda(lam_ref[...], lam_init)
        for h in range(n_heads):
            ln = slice(h * LANES, (h + 1) * LANES)
            r1 = slice((2 * h) * SAMPLE_PAD, (2 * h + 1) * SAMPLE_PAD)
            r2 = slice((2 * h + 1) * SAMPLE_PAD, (2 * h + 2) * SAMPLE_PAD)
            o1 = acc_scr[r1, :] / l_scr[r1, :1]
            o2 = acc_scr[r2, :] / l_scr[r2, :1]
            a_ref[:, ln] = _rms(o1 - lam * o2, ag_ref[...]) * (1.0 - lam_init)


def _decode_attention(page_table, cache_k, cache_v, layer, z_s, qg2, kg2, ag, lambda_qk, slopes, lam_init, dec_seq):
    pps = DEC_PAGES_PER_STEP
    dec_batch, n_pages = page_table.shape
    n_heads = slopes.shape[0]
    assert n_heads == SAMPLE_PAD and n_heads & (n_heads - 1) == 0
    att_w = n_heads * V_DIM_A
    past_len = n_pages * PAGE_SIZE
    rows = 2 * n_heads * SAMPLE_PAD
    page_rows = PAGE_SIZE * n_heads
    slope_rows = jnp.broadcast_to(jnp.repeat(slopes, 2 * SAMPLE_PAD).reshape(rows, 1), (rows, LANES))
    meta_blocks = N_META // SAMPLE_PAD

    def page_spec(p):
        return pl.BlockSpec((1, 1, PAGE_SIZE, n_heads, V_DIM_A),
                            lambda b, st, pt, p=p: (layer, pt[b, st * pps + p], 0, 0, 0))

    in_specs = [page_spec(p) for p in range(pps)] + [page_spec(p) for p in range(pps)] + [
        pl.BlockSpec((SAMPLE_PAD, att_w), lambda b, st, pt: (meta_blocks + b, 0)),
        pl.BlockSpec((SAMPLE_PAD, att_w), lambda b, st, pt: (meta_blocks + b, 1)),
        pl.BlockSpec((SAMPLE_PAD, att_w), lambda b, st, pt: (meta_blocks + b, 2)),
        pl.BlockSpec((1, LANES), lambda b, st, pt: (0, 0)),
        pl.BlockSpec((1, LANES), lambda b, st, pt: (0, 0)),
        pl.BlockSpec((1, LANES), lambda b, st, pt: (0, 0)),
        pl.BlockSpec(lambda_qk.shape, lambda b, st, pt: (0, 0)),
        pl.BlockSpec((rows, LANES), lambda b, st, pt: (0, 0)),
    ]
    kernel = functools.partial(_decode_kernel, lam_init=lam_init, past_len=float(past_len),
                               n_heads=n_heads, dec_seq=dec_seq)
    grid_spec = pltpu.PrefetchScalarGridSpec(
        num_scalar_prefetch=1,
        grid=(dec_batch, n_pages // pps),
        in_specs=in_specs,
        out_specs=[
            pl.BlockSpec((SAMPLE_PAD, att_w), lambda b, st, pt: (b, 0)),
            pl.BlockSpec((SAMPLE_PAD, att_w), lambda b, st, pt: (b, 0)),
        ],
        scratch_shapes=[
            pltpu.VMEM((rows, V_DIM_A), F32),
            pltpu.VMEM((LANES, V_DIM_A), F32),
            pltpu.VMEM((LANES, V_DIM_A), F32),
            pltpu.VMEM((rows, pps * page_rows), F32),
            pltpu.VMEM((rows, LANES), F32),
            pltpu.VMEM((rows, LANES), F32),
            pltpu.VMEM((rows, V_DIM_A), F32),
        ],
    )
    return pl.pallas_call(
        kernel,
        grid_spec=grid_spec,
        out_shape=[
            jax.ShapeDtypeStruct((dec_batch * SAMPLE_PAD, att_w), F32),
            jax.ShapeDtypeStruct((dec_batch * SAMPLE_PAD, att_w), F32),
        ],
        compiler_params=_params("parallel", "arbitrary"),
        name="decode_attn",
    )(page_table, *([cache_k] * pps), *([cache_v] * pps), z_s, z_s, z_s, qg2, kg2, ag, lambda_qk, slope_rows)


def _log_sigmoid(x):
    return jnp.minimum(x, 0.0) - jnp.log(1.0 + jnp.exp(-jnp.abs(x)))


def _gate_columns(g, gb, hm):
    lane = lax.broadcasted_iota(jnp.int32, g.shape, 1)
    gg = g + gb
    ig = jnp.sum(jnp.where(lane == hm, gg, 0.0), axis=-1, keepdims=True)
    fg = jnp.sum(jnp.where(lane == hm + N_HEADS_M, gg, 0.0), axis=-1, keepdims=True)
    return ig, _log_sigmoid(fg)


def _conv_silu(ubuf_ref, start, rows, cw, cb):
    acc = cb
    for j in range(CONV_W):
        acc = acc + ubuf_ref[start + 5 + j:start + 5 + j + rows, :] * cw[j:j + 1, :]
    return acc * jax.nn.sigmoid(acc)


def _mlstm_chunk(q, k, v, lf_c, ig_c, c0, n0, m0, want_h):
    L = q.shape[0]
    row = lax.broadcasted_iota(jnp.int32, (L, L), 0)
    col = lax.broadcasted_iota(jnp.int32, (L, L), 1)
    eye = row == col
    tril = col <= row
    lf_r = jnp.sum(jnp.where(eye, lf_c, 0.0), axis=0, keepdims=True)
    ig_r = jnp.sum(jnp.where(eye, ig_c, 0.0), axis=0, keepdims=True)
    b_c = jnp.sum(jnp.where(tril, lf_r, 0.0), axis=1, keepdims=True)
    b_r = jnp.sum(jnp.where(row <= col, lf_c, 0.0), axis=0, keepdims=True)
    d = jnp.where(tril, b_c - b_r + ig_r, NEG_BIG)
    inter = b_c + m0
    m_c = jnp.maximum(inter, jnp.max(d, axis=1, keepdims=True))
    b_l = b_c[L - 1:L]
    m_end = m_c[L - 1:L]
    w_c = jnp.exp(b_l + m0 - m_end)
    w_s = jnp.exp(b_l - b_c + ig_c - m_end)
    kw = k.astype(F32) * w_s
    c1 = w_c * c0 + _dot_tn(kw.astype(BF16), v)
    n1 = w_c * n0 + jnp.sum(kw, axis=0, keepdims=True)
    if not want_h:
        return c1, n1, m_end, None
    s = _dot_nt(q, k) * jnp.exp(d - m_c)
    w_i = jnp.exp(inter - m_c)
    num = w_i * _dot(q, c0.astype(BF16)) + _dot(s.astype(BF16), v)
    den = w_i * jnp.sum(q.astype(F32) * n0, axis=1, keepdims=True) + jnp.sum(s, axis=1, keepdims=True)
    h = num / jnp.maximum(jnp.abs(den), jnp.exp(-m_c))
    return c1, n1, m_end, h


def _mlstm_prompt_kernel(u_ref, vm_ref, og_ref, gt_ref, um_ref, vmm_ref, gtm_ref,
                         cw_ref, cb_ref, wq_ref, wk_ref, gb_ref, go_ref,
                         mo_ref, c_ref, n_ref, m_ref, cv_ref,
                         ubuf, q_scr, k_scr, v_scr, ig_scr, lf_scr, c_scr, n_scr, m_scr, *, seq):
    L = MLSTM_CHUNK
    hm = pl.program_id(1)
    d = u_ref.shape[1]
    n_chunks = seq // L
    cw = cw_ref[...]
    cb = cb_ref[...]
    wq = wq_ref[0].astype(BF16)
    wk = wk_ref[0].astype(BF16)
    kscale = d ** -0.5

    ubuf[0:SUBLANES, :] = jnp.zeros((SUBLANES, d), F32)
    ubuf[SUBLANES:SUBLANES + N_META, :] = um_ref[...]
    ubuf[SUBLANES + N_META:, :] = u_ref[...]

    def put(dst0, rows, ua, vrows, g):
        ub = ua.astype(BF16)
        q_scr[dst0:dst0 + rows, :] = _dot(ub, wq).astype(BF16)
        k_scr[dst0:dst0 + rows, :] = (_dot(ub, wk) * kscale).astype(BF16)
        v_scr[dst0:dst0 + rows, :] = vrows.astype(BF16)
        ig, lf = _gate_columns(g, gb_ref[...], hm)
        ig_scr[dst0:dst0 + rows, :] = ig
        lf_scr[dst0:dst0 + rows, :] = lf

    q_scr[0:L, :] = jnp.zeros((L, d), BF16)
    k_scr[0:L, :] = jnp.zeros((L, d), BF16)
    v_scr[0:L, :] = jnp.zeros((L, d), BF16)
    ig_scr[0:L, :] = jnp.full((L, 1), NEG_BIG, F32)
    lf_scr[0:L, :] = jnp.zeros((L, 1), F32)
    put(0, N_META, _conv_silu(ubuf, 0, N_META, cw, cb), vmm_ref[...], gtm_ref[...])
    piece = 2 * L
    for r0 in range(0, seq, piece):
        put(L + r0, piece, _conv_silu(ubuf, N_META + r0, piece, cw, cb),
            vm_ref[r0:r0 + piece, :], gt_ref[r0:r0 + piece, :])

    c1, n1, m1, _ = _mlstm_chunk(q_scr[0:L, :], k_scr[0:L, :], v_scr[0:L, :], lf_scr[0:L, :], ig_scr[0:L, :],
                                 jnp.zeros((d, d), F32), jnp.zeros((1, d), F32), jnp.zeros((1, 1), F32), False)
    c_scr[...] = c1
    n_scr[...] = n1
    m_scr[...] = jnp.broadcast_to(m1, m_scr.shape)

    def body(c, carry):
        r0 = pl.multiple_of((c + 1) * L, L)
        rs = pl.ds(r0, L)
        c1, n1, m1, h = _mlstm_chunk(q_scr[rs, :], k_scr[rs, :], v_scr[rs, :], lf_scr[rs, :], ig_scr[rs, :],
                                     c_scr[...], n_scr[...], m_scr[:, :1], True)
        c_scr[...] = c1
        n_scr[...] = n1
        m_scr[...] = jnp.broadcast_to(m1, m_scr.shape)
        os = pl.ds(pl.multiple_of(c * L, L), L)
        mo_ref[os, :] = (_rms(h, go_ref[...]) * jax.nn.sigmoid(og_ref[os, :])).astype(BF16)
        return carry

    lax.fori_loop(0, n_chunks, body, 0)
    c_ref[0] = c_scr[...]
    n_ref[0] = n_scr[...]
    m_ref[0] = m_scr[...]
    total = SUBLANES + N_META + seq
    cv_ref[0] = ubuf[total - (CONV_W - 1):total, :]


def _mlstm_prompt(z, gates, z_s, gates_s, conv_w, conv_b, w_qm, w_km, gbias, g_out, batch, seq, col0):
    d = w_qm.shape[1]
    nh = w_qm.shape[0]
    cb0 = col0 // d
    width = nh * d
    kernel = functools.partial(_mlstm_prompt_kernel, seq=seq)
    t_scr = MLSTM_CHUNK + seq
    return pl.pallas_call(
        kernel,
        grid=(batch, nh),
        in_specs=[
            pl.BlockSpec((seq, d), lambda b, h: (b, cb0 + h)),
            pl.BlockSpec((seq, d), lambda b, h: (b, cb0 + nh + h)),
            pl.BlockSpec((seq, d), lambda b, h: (b, cb0 + 2 * nh + h)),
            pl.BlockSpec((seq, LANES), lambda b, h: (b, 0)),
            pl.BlockSpec((N_META, d), lambda b, h: (0, cb0 + h)),
            pl.BlockSpec((N_META, d), lambda b, h: (0, cb0 + nh + h)),
            pl.BlockSpec((N_META, LANES), lambda b, h: (0, 0)),
            pl.BlockSpec((CONV_W, d), lambda b, h: (0, h)),
            pl.BlockSpec((1, d), lambda b, h: (0, h)),
            pl.BlockSpec((1, d, d), lambda b, h: (h, 0, 0)),
            pl.BlockSpec((1, d, d), lambda b, h: (h, 0, 0)),
            pl.BlockSpec((1, LANES), lambda b, h: (0, 0)),
            pl.BlockSpec((1, d), lambda b, h: (0, 0)),
        ],
        out_specs=[
            pl.BlockSpec((seq, d), lambda b, h: (b, h)),
            pl.BlockSpec((1, d, d), lambda b, h: (b * nh + h, 0, 0)),
            pl.BlockSpec((1, 1, d), lambda b, h: (b * nh + h, 0, 0)),
            pl.BlockSpec((1, 1, LANES), lambda b, h: (b * nh + h, 0, 0)),
            pl.BlockSpec((1, CONV_W - 1, d), lambda b, h: (b, 0, h)),
        ],
        out_shape=[
            jax.ShapeDtypeStruct((batch * seq, width), BF16),
            jax.ShapeDtypeStruct((batch * nh, d, d), F32),
            jax.ShapeDtypeStruct((batch * nh, 1, d), F32),
            jax.ShapeDtypeStruct((batch * nh, 1, LANES), F32),
            jax.ShapeDtypeStruct((batch, CONV_W - 1, width), F32),
        ],
        scratch_shapes=[
            pltpu.VMEM((SUBLANES + N_META + seq, d), F32),
            pltpu.VMEM((t_scr, d), BF16),
            pltpu.VMEM((t_scr, d), BF16),
            pltpu.VMEM((t_scr, d), BF16),
            pltpu.VMEM((t_scr, 1), F32),
            pltpu.VMEM((t_scr, 1), F32),
            pltpu.VMEM((d, d), F32),
            pltpu.VMEM((1, d), F32),
            pltpu.VMEM((1, LANES), F32),
        ],
        compiler_params=_params("parallel", "parallel"),
        name="mlstm_prompt",
    )(z, z, z, gates, z_s, z_s, gates_s, conv_w, conv_b, w_qm, w_km, gbias, g_out)


def _mlstm_sample_kernel(u_ref, vm_ref, og_ref, gt_ref, sc_ref, c0_ref, n0_ref, m0_ref,
                         cw_ref, cb_ref, wq_ref, wk_ref, gb_ref, go_ref,
                         mo_ref, c_ref, n_ref, m_ref, cv_ref,
                         ubuf, q_scr, k_scr, v_scr, ig_scr, lf_scr, *, dec_seq):
    L = MLSTM_CHUNK
    P = SAMPLE_PAD
    hm = pl.program_id(1)
    d = u_ref.shape[1]
    wq = wq_ref[0].astype(BF16)
    wk = wk_ref[0].astype(BF16)

    ubuf[0:SUBLANES, :] = jnp.zeros((SUBLANES, d), F32)
    ubuf[SUBLANES - (CONV_W - 1):SUBLANES, :] = sc_ref[0, 0]
    ubuf[SUBLANES:SUBLANES + P, :] = u_ref[...]
    ubuf[SUBLANES + P:, :] = jnp.zeros((SUBLANES, d), F32)

    ua = _conv_silu(ubuf, 0, P, cw_ref[...], cb_ref[...]).astype(BF16)
    q_scr[...] = jnp.zeros((L, d), BF16)
    k_scr[...] = jnp.zeros((L, d), BF16)
    v_scr[...] = jnp.zeros((L, d), BF16)
    q_scr[0:2 * P, :] = jnp.concatenate([_dot(ua, wq), jnp.zeros((P, d), F32)], axis=0).astype(BF16)
    k_scr[0:2 * P, :] = jnp.concatenate([_dot(ua, wk) * d ** -0.5, jnp.zeros((P, d), F32)], axis=0).astype(BF16)
    v_scr[0:2 * P, :] = jnp.concatenate([vm_ref[...], jnp.zeros((P, d), F32)], axis=0).astype(BF16)
    ig, lf = _gate_columns(gt_ref[...], gb_ref[...], hm)
    tok = lax.broadcasted_iota(jnp.int32, (P, 1), 0)
    ig_scr[...] = jnp.full((L, 1), NEG_BIG, F32)
    lf_scr[...] = jnp.zeros((L, 1), F32)
    ig_scr[0:P, :] = jnp.where(tok < dec_seq, ig, NEG_BIG)
    lf_scr[0:P, :] = jnp.where(tok < dec_seq, lf, 0.0)

    c1, n1, m1, h = _mlstm_chunk(q_scr[...], k_scr[...], v_scr[...], lf_scr[...], ig_scr[...],
                                 c0_ref[0], n0_ref[0], m0_ref[0][:, :1], True)
    c_ref[0] = c1
    n_ref[0] = n1
    m_ref[0] = jnp.broadcast_to(m1, (1, LANES))
    mo_ref[...] = _rms(h[0:P], go_ref[...]) * jax.nn.sigmoid(og_ref[...])
    last = SUBLANES + dec_seq
    cv_ref[0, 0] = ubuf[last - (CONV_W - 1):last, :]


def _mlstm_sample(z_s, gates_s, state_conv, c0, n0, m0, conv_w, conv_b, w_qm, w_km, gbias, g_out,
                  dec_batch, dec_seq, col0):
    d = w_qm.shape[1]
    nh = w_qm.shape[0]
    cb0 = col0 // d
    width = nh * d
    mb = N_META // SAMPLE_PAD
    L = MLSTM_CHUNK
    kernel = functools.partial(_mlstm_sample_kernel, dec_seq=dec_seq)
    return pl.pallas_call(
        kernel,
        grid=(dec_batch, nh),
        in_specs=[
            pl.BlockSpec((SAMPLE_PAD, d), lambda b, h: (mb + b, cb0 + h)),
            pl.BlockSpec((SAMPLE_PAD, d), lambda b, h: (mb + b, cb0 + nh + h)),
            pl.BlockSpec((SAMPLE_PAD, d), lambda b, h: (mb + b, cb0 + 2 * nh + h)),
            pl.BlockSpec((SAMPLE_PAD, LANES), lambda b, h: (mb + b, 0)),
            pl.BlockSpec((1, 1, CONV_W - 1, d), lambda b, h: (0, b, 0, h)),
            pl.BlockSpec((1, d, d), lambda b, h: (b * nh + h, 0, 0)),
            pl.BlockSpec((1, 1, d), lambda b, h: (b * nh + h, 0, 0)),
            pl.BlockSpec((1, 1, LANES), lambda b, h: (b * nh + h, 0, 0)),
            pl.BlockSpec((CONV_W, d), lambda b, h: (0, h)),
            pl.BlockSpec((1, d), lambda b, h: (0, h)),
            pl.BlockSpec((1, d, d), lambda b, h: (h, 0, 0)),
            pl.BlockSpec((1, d, d), lambda b, h: (h, 0, 0)),
            pl.BlockSpec((1, LANES), lambda b, h: (0, 0)),
            pl.BlockSpec((1, d), lambda b, h: (0, 0)),
        ],
        out_specs=[
            pl.BlockSpec((SAMPLE_PAD, d), lambda b, h: (b, h)),
            pl.BlockSpec((1, d, d), lambda b, h: (b * nh + h, 0, 0)),
            pl.BlockSpec((1, 1, d), lambda b, h: (b * nh + h, 0, 0)),
            pl.BlockSpec((1, 1, LANES), lambda b, h: (b * nh + h, 0, 0)),
            pl.BlockSpec((1, 1, CONV_W - 1, d), lambda b, h: (0, b, 0, h)),
        ],
        out_shape=[
            jax.ShapeDtypeStruct((dec_batch * SAMPLE_PAD, width), F32),
            jax.ShapeDtypeStruct((dec_batch * nh, d, d), F32),
            jax.ShapeDtypeStruct((dec_batch * nh, 1, d), F32),
            jax.ShapeDtypeStruct((dec_batch * nh, 1, LANES), F32),
            jax.ShapeDtypeStruct((1, dec_batch, CONV_W - 1, width), F32),
        ],
        scratch_shapes=[
            pltpu.VMEM((3 * SUBLANES, d), F32),
            pltpu.VMEM((L, d), BF16),
            pltpu.VMEM((L, d), BF16),
            pltpu.VMEM((L, d), BF16),
            pltpu.VMEM((L, 1), F32),
            pltpu.VMEM((L, 1), F32),
        ],
        compiler_params=_params("parallel", "parallel"),
        name="mlstm_sample",
    )(z_s, z_s, z_s, gates_s, state_conv, c0, n0, m0, conv_w, conv_b, w_qm, w_km, gbias, g_out)


def _layer(l, x_p, x_s, cache_k, cache_v, state_C, state_n, state_m, state_conv, page_table,
           norm_mix, w_in, q_norm, k_norm, lambda_qk, attn_out_norm, conv_w, conv_b, w_qm, w_km,
           gate_bias, mlstm_out_norm, w_out, norm_ffn, w_up, w_down, *, batch, seq, dec_batch, dec_seq):
    d_model = x_p.shape[1]
    nh_m = w_qm.shape[0]
    d_m = w_qm.shape[1]
    mlstm_w = nh_m * d_m
    att_w = d_model - mlstm_w
    n_heads_a = att_w // V_DIM_A
    main_w = 3 * att_w + 3 * mlstm_w
    lam_init = 0.8 - 0.6 * math.exp(-0.3 * l)
    slopes = 2.0 ** (-8.0 * jnp.arange(1, n_heads_a + 1, dtype=F32) / n_heads_a)

    w_main = w_in
    wg = jnp.pad(w_in[:, main_w:], ((0, 0), (0, LANES - 2 * nh_m)))
    wgh = wg.astype(BF16)
    wgl = (wg - wgh.astype(F32)).astype(BF16)
    g_mix = norm_mix.reshape(1, d_model)
    qg2 = jnp.tile(q_norm, 2).reshape(1, V_DIM_A)
    kg2 = jnp.tile(k_norm, 2).reshape(1, V_DIM_A)
    ag = attn_out_norm.reshape(1, V_DIM_A)
    gbias = jnp.pad(gate_bias.reshape(1, 2 * nh_m), ((0, 0), (0, LANES - 2 * nh_m)))
    cb = conv_b.reshape(1, mlstm_w)
    g_out = mlstm_out_norm.reshape(1, d_m)
    wo_a = w_out[:att_w]
    wo_m = w_out[att_w:]
    g_ffn = norm_ffn.reshape(1, d_model)
    wu = w_up
    wd = w_down

    rows_s = x_s.shape[0]
    z, gates = _inproj(x_p, g_mix, w_main, wgh, wgl, tm=1024, tn=512)
    z_s, gates_s = _inproj(x_s, g_mix, w_main, wgh, wgl, tm=rows_s, tn=512)

    k_p, v_p, a_p = _prompt_attention(z, z_s, qg2, kg2, ag, lambda_qk, slopes, lam_init, batch, seq)
    a_s, ks = _decode_attention(page_table, cache_k, cache_v, l, z_s, qg2, kg2, ag, lambda_qk, slopes,
                                lam_init, dec_seq)

    m_p, c_p, n_p, mm_p, cv_p = _mlstm_prompt(z, gates, z_s, gates_s, conv_w, cb, w_qm, w_km, gbias, g_out,
                                              batch, seq, 3 * att_w)
    c0 = state_C.reshape(dec_batch * nh_m, d_m, d_m)
    n0 = state_n.reshape(dec_batch * nh_m, 1, d_m)
    m0 = jnp.broadcast_to(state_m.reshape(dec_batch * nh_m, 1, 1), (dec_batch * nh_m, 1, LANES))
    m_s, c_s, n_s, mm_s, cv_s = _mlstm_sample(z_s, gates_s, state_conv, c0, n0, m0, conv_w, cb, w_qm, w_km,
                                              gbias, g_out, dec_batch, dec_seq, 3 * att_w)

    x_sample_rows = x_s[N_META:]
    x1_p = _outproj(x_p, a_p, m_p, wo_a, wo_m, tm=1024, tn=512)
    x1_s = _outproj(x_sample_rows, a_s, m_s, wo_a, wo_m, tm=x_sample_rows.shape[0], tn=512)
    y_p = _ffn(x1_p, g_ffn, wu, wd, tm=1024, tf=512)
    y_s = _ffn(x1_s, g_ffn, wu, wd, tm=x1_s.shape[0], tf=512)

    def unpad(a):
        return a.reshape(dec_batch, SAMPLE_PAD, a.shape[-1])[:, :dec_seq]

    outs = dict(
        y_p=y_p, y_s=unpad(y_s),
        k_p=k_p.reshape(batch, seq + N_META, n_heads_a, V_DIM_A),
        v_p=v_p.reshape(batch, seq + N_META, n_heads_a, V_DIM_A),
        c_p=c_p.reshape(batch, nh_m, d_m, d_m), n_p=n_p.reshape(batch, nh_m, d_m),
        m_p=mm_p[:, 0, 0].reshape(batch, nh_m), cv_p=cv_p,
        k_s=unpad(ks).reshape(dec_batch, dec_seq, n_heads_a, V_DIM_A),
        v_s=unpad(z_s[N_META:, 2 * att_w:3 * att_w]).reshape(dec_batch, dec_seq, n_heads_a, V_DIM_A),
        c_s=c_s.reshape(dec_batch, nh_m, d_m, d_m), n_s=n_s.reshape(dec_batch, nh_m, d_m),
        m_s=mm_s[:, 0, 0].reshape(dec_batch, nh_m), cv_s=cv_s[0],
    )
    return outs


def kernel(x_prompt, x_sample, cache_k, cache_v, state_C, state_n, state_m, state_conv, page_table, meta_tokens, norm_mix, w_in, q_norm, k_norm, lambda_qk, attn_out_norm, conv_w, conv_b, w_qm, w_km, gate_bias, mlstm_out_norm, w_out, norm_ffn, w_up, w_down):
    batch, seq, d_model = x_prompt.shape
    dec_batch, dec_seq, _ = x_sample.shape
    depth = w_in.shape[0]
    assert depth == 1, "prompt rows are kept without the meta tokens, which is only valid for one layer"
    assert dec_seq <= SAMPLE_PAD and seq % ATT_QBLOCK == 0 and seq % (2 * MLSTM_CHUNK) == 0
    assert ATT_QBLOCK % ATT_KBLOCK == 0

    x_p = x_prompt.reshape(batch * seq, d_model)
    xs_pad = jnp.pad(x_sample, ((0, 0), (0, SAMPLE_PAD - dec_seq), (0, 0))).reshape(dec_batch * SAMPLE_PAD, d_model)
    x_s = jnp.concatenate([meta_tokens.astype(x_prompt.dtype), xs_pad], axis=0)

    l = 0
    o = _layer(l, x_p, x_s, cache_k, cache_v, state_C[l], state_n[l], state_m[l], state_conv[l:l + 1],
               page_table, norm_mix[l], w_in[l], q_norm[l], k_norm[l], lambda_qk[l], attn_out_norm[l],
               conv_w[l], conv_b[l], w_qm[l], w_km[l], gate_bias[l], mlstm_out_norm[l], w_out[l],
               norm_ffn[l], w_up[l], w_down[l], batch=batch, seq=seq, dec_batch=dec_batch, dec_seq=dec_seq)
    st = lambda a: a[None]
    return (o["y_p"].reshape(batch, seq, d_model), o["y_s"],
            st(o["k_p"]), st(o["v_p"]), st(o["c_p"]), st(o["n_p"]), st(o["m_p"]), st(o["cv_p"]),
            st(o["k_s"]), st(o["v_s"]), st(o["c_s"]), st(o["n_s"]), st(o["m_s"]), st(o["cv_s"]))
```
